```python
import math
import jax, jax.numpy as jnp
from jax import lax
import numpy as np


D_MODEL = 1024
BATCH = 2
SEQ = 8192
DEPTH = 2

CHUNK = 64
N_META = 16
Q_BLOCK = 128
N_MIXERS = 2
DIFF_HEADS = 8
DIFF_HEAD_DIM = D_MODEL // (2 * DIFF_HEADS)
SB_HEADS = 16
SB_HEAD_DIM = D_MODEL // SB_HEADS
N_GROUPS = 4
EXPERTS_PER_GROUP = 8
N_EXPERTS = N_GROUPS * EXPERTS_PER_GROUP
EXPERT_TOP_K = 2
D_EXPERT = 512
NORM_EPS = 1e-6
N_DIFF_LAYERS = (DEPTH + 1) // 2
N_SB_LAYERS = DEPTH // 2

kernel_name = 'hybrid_diff_stickbreak_hmoe_chunk_causal'


def rms_norm(x, g):
    xf = x.astype(jnp.float32)
    y = xf * lax.rsqrt(jnp.mean(xf * xf, axis=-1, keepdims=True) + NORM_EPS)
    return (y * g.astype(jnp.float32)).astype(x.dtype)


def chunk_ids(length):
    p = jnp.arange(length)
    return jnp.where(p < N_META, 0, (p - N_META) // CHUNK + 1)


def alibi_slopes(n_heads):
    return jnp.exp2(-8.0 * (jnp.arange(n_heads, dtype=jnp.float32) + 1.0) / n_heads)


def diff_attention(h, w_qkv, q_gain, k_gain, lq1, lk1, lq2, lk2, sub_gain, w_o, lambda_init):
    B, L, D = h.shape
    H, d = DIFF_HEADS, DIFF_HEAD_DIM
    q, k, v = jnp.split(h @ w_qkv, 3, axis=-1)
    q = rms_norm(q.reshape(B, L, H, 2, d), q_gain).transpose(0, 2, 3, 1, 4)
    k = rms_norm(k.reshape(B, L, H, 2, d), k_gain).transpose(0, 2, 3, 1, 4)
    v = v.reshape(B, L, H, 2 * d).transpose(0, 2, 1, 3)
    lam = (jnp.exp(jnp.sum(lq1.astype(jnp.float32) * lk1.astype(jnp.float32)))
           - jnp.exp(jnp.sum(lq2.astype(jnp.float32) * lk2.astype(jnp.float32))) + lambda_init)
    slopes = alibi_slopes(H)
    pos = jnp.arange(L)
    cid = chunk_ids(L)
    scale = d ** -0.5

    def block(i):
        start = i * Q_BLOCK
        qb = lax.dynamic_slice_in_dim(q, start, Q_BLOCK, axis=3)
        s = jnp.einsum('bhcqd,bhckd->bhcqk', qb, k).astype(jnp.float32) * scale
        tq = lax.dynamic_slice_in_dim(pos, start, Q_BLOCK)
        cq = lax.dynamic_slice_in_dim(cid, start, Q_BLOCK)
        dist = jnp.abs(tq[:, None] - pos[None, :]).astype(jnp.float32)
        bias = -slopes[:, None, None, None] * dist[None, None]
        allowed = cid[None, :] <= cq[:, None]
        p = jax.nn.softmax(jnp.where(allowed, s + bias, -jnp.inf), axis=-1)
        a = p[:, :, 0] - lam * p[:, :, 1]
        return jnp.einsum('bhqk,bhke->bhqe', a.astype(v.dtype), v)

    o = lax.map(block, jnp.arange(L // Q_BLOCK))
    o = o.transpose(1, 0, 3, 2, 4).reshape(B, L, H, 2 * d)
    o = rms_norm(o, sub_gain) * (1.0 - lambda_init)
    return o.reshape(B, L, H * 2 * d) @ w_o


def stick_breaking(h, w_qkv, w_o):
    B, L, D = h.shape
    H, d = SB_HEADS, SB_HEAD_DIM
    q, k, v = jnp.split(h @ w_qkv, 3, axis=-1)
    q, k, v = [t.reshape(B, L, H, d).transpose(0, 2, 1, 3) for t in (q, k, v)]
    pos = jnp.arange(L)
    scale = d ** -0.5

    def block(i):
        start = i * Q_BLOCK
        qb = lax.dynamic_slice_in_dim(q, start, Q_BLOCK, axis=2)
        z = jnp.einsum('bhqd,bhkd->bhqk', qb, k).astype(jnp.float32) * scale
        tq = lax.dynamic_slice_in_dim(pos, start, Q_BLOCK)
        before = pos[None, :] < tq[:, None]
        log_keep = jnp.where(before, jax.nn.log_sigmoid(-z), 0.0)
        between = lax.cumsum(log_keep, axis=3, reverse=True) - log_keep
        w = jnp.where(before, jnp.exp(jax.nn.log_sigmoid(z) + between), 0.0)
        return jnp.einsum('bhqk,bhkd->bhqd', w.astype(v.dtype), v)

    o = lax.map(block, jnp.arange(L // Q_BLOCK))
    o = o.transpose(1, 0, 3, 2, 4).reshape(B, L, H * d)
    return o @ w_o


def hier_moe(h, w_group, b_group, w_router, b_router, w_gate, w_up, w_down):
    B, L, D = h.shape
    xf = h.reshape(B * L, D)
    g_prob = jax.nn.softmax((xf @ w_group).astype(jnp.float32) + b_group.astype(jnp.float32), axis=-1)
    g_p, g_idx = lax.top_k(g_prob, 1)
    e_logits = ((xf @ w_router).astype(jnp.float32) + b_router.astype(jnp.float32))
    e_logits = e_logits.reshape(B * L, N_GROUPS, EXPERTS_PER_GROUP)
    e_in = jnp.take_along_axis(e_logits, g_idx[:, :, None], axis=1)[:, 0]
    e_val, e_idx = lax.top_k(e_in, EXPERT_TOP_K)
    weights = g_p * jax.nn.softmax(e_val, axis=-1)
    expert_id = g_idx * EXPERTS_PER_GROUP + e_idx
    combine = jnp.einsum('nk,nke->ne', weights,
                         jax.nn.one_hot(expert_id, N_EXPERTS, dtype=jnp.float32)).astype(xf.dtype)
    out = jnp.zeros_like(xf)
    for gi in range(N_GROUPS):
        sl = slice(gi * EXPERTS_PER_GROUP, (gi + 1) * EXPERTS_PER_GROUP)
        a = jnp.einsum('nd,edf->nef', xf, w_gate[sl])
        u = jnp.einsum('nd,edf->nef', xf, w_up[sl])
        hid = jax.nn.silu(a) * u * combine[:, sl, None]
        out = out + jnp.einsum('nef,efd->nd', hid, w_down[sl])
    return out.reshape(B, L, D)


def setup_inputs(seed: int = 0) -> dict:
    key = jax.random.key(seed)
    ks = jax.random.split(key, 24)
    f32 = jnp.float32
    D = D_MODEL
    nA, nB = N_DIFF_LAYERS, N_SB_LAYERS

    def nrm(k, shape, scale):
        return jax.random.normal(k, shape, f32) * scale

    return {
        'x': nrm(ks[0], (BATCH, SEQ, D), 1.0),
        'meta_tokens': nrm(ks[1], (N_META, D), 1.0),
        'norm_mix': 1.0 + nrm(ks[2], (DEPTH, D), 0.01),
        'norm_ffn': 1.0 + nrm(ks[3], (DEPTH, D), 0.01),
        'diff_w_qkv': nrm(ks[4], (nA, D, 3 * D), D ** -0.5),
        'diff_q_gain': 1.0 + nrm(ks[5], (nA, DIFF_HEAD_DIM), 0.01),
        'diff_k_gain': 1.0 + nrm(ks[6], (nA, DIFF_HEAD_DIM), 0.01),
        'diff_lambda_q1': nrm(ks[7], (nA, DIFF_HEAD_DIM), 0.1),
        'diff_lambda_k1': nrm(ks[8], (nA, DIFF_HEAD_DIM), 0.1),
        'diff_lambda_q2': nrm(ks[9], (nA, DIFF_HEAD_DIM), 0.1),
        'diff_lambda_k2': nrm(ks[10], (nA, DIFF_HEAD_DIM), 0.1),
        'diff_sub_gain': 1.0 + nrm(ks[11], (nA, 2 * DIFF_HEAD_DIM), 0.01),
        'diff_w_o': nrm(ks[12], (nA, D, D), D ** -0.5),
        'sb_w_qkv': nrm(ks[13], (nB, D, 3 * D), D ** -0.5),
        'sb_w_o': nrm(ks[14], (nB, D, D), D ** -0.5),
        'moe_w_group': nrm(ks[15], (DEPTH, D, N_GROUPS), D ** -0.5),
        'moe_b_group': nrm(ks[16], (DEPTH, N_GROUPS), 0.01),
        'moe_w_router': nrm(ks[17], (DEPTH, D, N_EXPERTS), D ** -0.5),
        'moe_b_router': nrm(ks[18], (DEPTH, N_EXPERTS), 0.01),
        'moe_w_gate': nrm(ks[19], (DEPTH, N_EXPERTS, D, D_EXPERT), D ** -0.5),
        'moe_w_up': nrm(ks[20], (DEPTH, N_EXPERTS, D, D_EXPERT), D ** -0.5),
        'moe_w_down': nrm(ks[21], (DEPTH, N_EXPERTS, D_EXPERT, D), D_EXPERT ** -0.5),
    }


def reference(x, meta_tokens, norm_mix, norm_ffn, diff_w_qkv, diff_q_gain, diff_k_gain,
              diff_lambda_q1, diff_lambda_k1, diff_lambda_q2, diff_lambda_k2, diff_sub_gain,
              diff_w_o, sb_w_qkv, sb_w_o, moe_w_group, moe_b_group, moe_w_router,
              moe_b_router, moe_w_gate, moe_w_up, moe_w_down):
    B, S, D = x.shape
    L = N_META + S
    Lp = -(-L // Q_BLOCK) * Q_BLOCK
    meta = jnp.broadcast_to(meta_tokens.astype(x.dtype)[None], (B, N_META, D))
    h = jnp.concatenate([meta, x, jnp.zeros((B, Lp - L, D), x.dtype)], axis=1)
    for i in range(DEPTH):
        j = i // N_MIXERS
        hn = rms_norm(h, norm_mix[i])
        if i % N_MIXERS == 0:
            lambda_init = 0.8 - 0.6 * math.exp(-0.3 * i)
            mix = diff_attention(hn, diff_w_qkv[j], diff_q_gain[j], diff_k_gain[j],
                                 diff_lambda_q1[j], diff_lambda_k1[j], diff_lambda_q2[j],
                                 diff_lambda_k2[j], diff_sub_gain[j], diff_w_o[j], lambda_init)
        else:
            mix = stick_breaking(hn, sb_w_qkv[j], sb_w_o[j])
        h = h + mix
        h = h + hier_moe(rms_norm(h, norm_ffn[i]), moe_w_group[i], moe_b_group[i],
                         moe_w_router[i], moe_b_router[i], moe_w_gate[i], moe_w_up[i],
                         moe_w_down[i])
    return h[:, N_META:N_META + S]
```

```python
import functools
import math

import jax
import jax.numpy as jnp
from jax import lax
from jax.experimental import pallas as pl
from jax.experimental.pallas import tpu as pltpu

F32 = jnp.float32
BF16 = jnp.bfloat16

D_MODEL = 1024
CHUNK = 64
CHUNK_SHIFT = 6
N_META = 16
Q_BLOCK = 128
N_MIXERS = 2
DIFF_HEADS = 8
DIFF_HEAD_DIM = D_MODEL // (2 * DIFF_HEADS)
SB_HEADS = 16
SB_HEAD_DIM = D_MODEL // SB_HEADS
N_GROUPS = 4
EXPERTS_PER_GROUP = 8
EXPERT_SHIFT = 3
N_EXPERTS = N_GROUPS * EXPERTS_PER_GROUP
D_EXPERT = 512
NORM_EPS = 1e-6

LANES = 128
MXU_DIM = 256
VMEM_LIMIT = 56 * 1024 * 1024

ROW_TILE = 640
TQ = 128
TK_BIG = 512
TK_SMALL = 128
SB_TK = 256
EXPERT_TILE = 256
GATHER_ROWS = 128
NEG_BIG = -1e30


def _cparams(n_axes):
    return pltpu.CompilerParams(
        dimension_semantics=("arbitrary",) * n_axes,
        vmem_limit_bytes=VMEM_LIMIT,
    )


def _rms(x, gain):
    ms = jnp.mean(x * x, axis=-1, keepdims=True)
    return x * lax.rsqrt(ms + NORM_EPS) * gain


def _norm_qkv_diff_kernel(h_ref, g_ref, w_ref, qg_ref, kg_ref, gm_ref, o_ref):
    D = D_MODEL
    xn = _rms(h_ref[...], g_ref[...]).astype(BF16)
    for part, gain_ref in ((0, qg_ref), (1, kg_ref)):
        y = jnp.dot(xn, w_ref[:, part * D:(part + 1) * D], preferred_element_type=F32)
        y2 = (y * y).astype(BF16)
        ss = jnp.concatenate(
            [jnp.dot(y2[:, c * MXU_DIM:(c + 1) * MXU_DIM], gm_ref[...],
                     preferred_element_type=F32) for c in range(D // MXU_DIM)], axis=1)
        yn = y * lax.rsqrt(ss * (1.0 / DIFF_HEAD_DIM) + NORM_EPS) * gain_ref[...]
        o_ref[:, part * D:(part + 1) * D] = yn.astype(BF16)
    v = jnp.dot(xn, w_ref[:, 2 * D:3 * D], preferred_element_type=F32)
    o_ref[:, 2 * D:3 * D] = v.astype(BF16)


def _norm_qkv_sb_kernel(h_ref, g_ref, w_ref, o_ref):
    D = D_MODEL
    xn = _rms(h_ref[...], g_ref[...]).astype(BF16)
    for part in range(3):
        y = jnp.dot(xn, w_ref[:, part * D:(part + 1) * D], preferred_element_type=F32)
        o_ref[:, part * D:(part + 1) * D] = y.astype(BF16)


def _norm_qkv(h, gain, w_bf16, qk=None):
    n, d = h.shape
    grid = (n // ROW_TILE,)
    row = pl.BlockSpec((ROW_TILE, d), lambda i: (i, 0))
    vec = pl.BlockSpec((1, d), lambda i: (0, 0))
    wspec = pl.BlockSpec((d, 3 * d), lambda i: (0, 0))
    out = pl.BlockSpec((ROW_TILE, 3 * d), lambda i: (i, 0))
    out_shape = jax.ShapeDtypeStruct((n, 3 * d), BF16)
    if qk is None:
        return pl.pallas_call(
            _norm_qkv_sb_kernel, grid=grid, in_specs=[row, vec, wspec], out_specs=out,
            out_shape=out_shape, compiler_params=_cparams(1), name="norm_qkv_sb",
        )(h, gain, w_bf16)
    qg, kg, gm = qk
    gspec = pl.BlockSpec((MXU_DIM, MXU_DIM), lambda i: (0, 0))
    return pl.pallas_call(
        _norm_qkv_diff_kernel, grid=grid, in_specs=[row, vec, wspec, vec, vec, gspec],
        out_specs=out, out_shape=out_shape, compiler_params=_cparams(1), name="norm_qkv_diff",
    )(h, gain, w_bf16, qg, kg, gm)


def _diff_attn_kernel(slopes_ref, q_ref, k_ref, v_ref, lam_ref, sg_ref, o_ref,
                      acc0, acc1, m0, m1, *, seq, lambda_init):
    hd = pl.program_id(1)
    slope = slopes_ref[hd]
    lp = lam_ref[...]
    lam = (jnp.exp(jnp.sum(lp[0:1] * lp[1:2], axis=-1, keepdims=True))
           - jnp.exp(jnp.sum(lp[2:3] * lp[3:4], axis=-1, keepdims=True)) + lambda_init)
    ones = jnp.ones((TK_BIG, LANES), BF16)
    lane = lax.broadcasted_iota(jnp.int32, (TQ, LANES), 1)
    dn = (((1,), (1,)), ((), ()))

    def q_body(qi, carry):
        qs = pl.multiple_of(qi * TQ, TQ)
        q = q_ref[pl.ds(qs, TQ), :]
        zero = jnp.zeros_like(q)
        qc = (jnp.where(lane < DIFF_HEAD_DIM, q, zero), jnp.where(lane >= DIFF_HEAD_DIM, q, zero))
        qpos = qs + lax.broadcasted_iota(jnp.int32, (TQ, 1), 0)
        bound = ((qpos + (CHUNK - N_META)) >> CHUNK_SHIFT) * CHUNK + N_META
        for acc, m in ((acc0, m0), (acc1, m1)):
            acc[...] = jnp.zeros_like(acc)
            m[...] = jnp.full_like(m, NEG_BIG)

        def tile(kstart, tk, masked):
            k = k_ref[pl.ds(kstart, tk), :]
            v = v_ref[pl.ds(kstart, tk), :]
            vext = jnp.concatenate([v, ones[:tk]], axis=1)
            kpos = kstart + lax.broadcasted_iota(jnp.int32, (1, tk), 1)
            rel = (kpos - qs).astype(F32) * slope
            if masked:
                ahead = jnp.maximum(kpos - qpos, 0).astype(F32) * (2.0 * slope)
                allowed = kpos < bound
            for c, (acc, m) in enumerate(((acc0, m0), (acc1, m1))):
                s = lax.dot_general(qc[c], k, dn, preferred_element_type=F32) + rel
                if masked:
                    s = jnp.where(allowed, s - ahead, -jnp.inf)
                m_old = m[...]
                m_new = jnp.maximum(m_old, jnp.max(s, axis=1, keepdims=True))
                alpha = jnp.exp(m_old - m_new)
                p = jnp.exp(s - m_new).astype(BF16)
                acc[...] = acc[...] * alpha + jnp.dot(p, vext, preferred_element_type=F32)
                m[...] = m_new

        n_big = qs // TK_BIG
        def big_body(j, c):
            tile(pl.multiple_of(j * TK_BIG, TK_BIG), TK_BIG, False)
            return c
        lax.fori_loop(0, n_big, big_body, 0)
        base = n_big * TK_BIG
        def small_body(j, c):
            tile(pl.multiple_of(base + j * TK_SMALL, TK_SMALL), TK_SMALL, False)
            return c
        lax.fori_loop(0, (qs - base) // TK_SMALL, small_body, 0)
        tile(qs, TK_SMALL, True)
        @pl.when(qs + TK_SMALL < seq)
        def _():
            tile(pl.multiple_of(qs + TK_SMALL, TK_SMALL), TK_SMALL, True)

        a0 = acc0[...]
        a1 = acc1[...]
        o = a0[:, :LANES] / a0[:, LANES:] - lam * (a1[:, :LANES] / a1[:, LANES:])
        o = _rms(o, sg_ref[...]) * (1.0 - lambda_init)
        o_ref[pl.ds(qs, TQ), :] = o.astype(o_ref.dtype)
        return carry

    lax.fori_loop(0, seq // TQ, q_body, 0)


def _diff_attention(qkv, slopes, lam_params, sub_gain, batch, seq, lambda_init):
    H = DIFF_HEADS
    kern = functools.partial(_diff_attn_kernel, seq=seq, lambda_init=lambda_init)
    blk = lambda off: pl.BlockSpec((seq, LANES), lambda b, h, s: (b, off + h))
    return pl.pallas_call(
        kern,
        grid_spec=pltpu.PrefetchScalarGridSpec(
            num_scalar_prefetch=1, grid=(batch, H),
            in_specs=[blk(0), blk(H), blk(2 * H),
                      pl.BlockSpec((4, DIFF_HEAD_DIM), lambda b, h, s: (0, 0)),
                      pl.BlockSpec((1, LANES), lambda b, h, s: (0, 0))],
            out_specs=pl.BlockSpec((seq, LANES), lambda b, h, s: (b, h)),
            scratch_shapes=[pltpu.VMEM((TQ, 2 * LANES), F32), pltpu.VMEM((TQ, 2 * LANES), F32),
                            pltpu.VMEM((TQ, 1), F32), pltpu.VMEM((TQ, 1), F32)]),
        out_shape=jax.ShapeDtypeStruct((batch * seq, D_MODEL), BF16),
        compiler_params=_cparams(2), name="diff_attn",
    )(slopes, qkv, qkv, qkv, lam_params, sub_gain)


def _sb_attn_kernel(q_ref, k_ref, v_ref, o_ref, acc, run0, run1, *, seq):
    d = SB_HEAD_DIM
    lane = lax.broadcasted_iota(jnp.int32, (TQ, LANES), 1)
    dn = (((1,), (1,)), ((), ()))

    def suffix_ones(tk):
        r = lax.broadcasted_iota(jnp.int32, (tk, tk), 0)
        c = lax.broadcasted_iota(jnp.int32, (tk, tk), 1)
        return jnp.where(r > c, 1.0, 0.0).astype(BF16)

    upper = {tk: suffix_ones(tk) for tk in (TQ, SB_TK)}
    vlow = {tk: lax.broadcasted_iota(jnp.int32, (tk, LANES), 1) < d for tk in (TQ, SB_TK)}

    def q_body(qi, carry):
        qs = pl.multiple_of(qi * TQ, TQ)
        q = q_ref[pl.ds(qs, TQ), :]
        zero = jnp.zeros_like(q)
        qh = (jnp.where(lane < d, q, zero), jnp.where(lane >= d, q, zero))
        qpos = qs + lax.broadcasted_iota(jnp.int32, (TQ, 1), 0)
        acc[...] = jnp.zeros_like(acc)
        run0[...] = jnp.zeros_like(run0)
        run1[...] = jnp.zeros_like(run1)

        def tile(kstart, tk, masked):
            k = k_ref[pl.ds(kstart, tk), :]
            v = v_ref[pl.ds(kstart, tk), :]
            vzero = jnp.zeros_like(v)
            vh = (jnp.where(vlow[tk], v, vzero), jnp.where(vlow[tk], vzero, v))
            if masked:
                before = (kstart + lax.broadcasted_iota(jnp.int32, (1, tk), 1)) < qpos
            total = acc[...]
            for hh, run in enumerate((run0, run1)):
                z = lax.dot_general(qh[hh], k, dn, preferred_element_type=F32)
                lk = jnp.minimum(-z, 0.0) - jnp.log(1.0 + jnp.exp(-jnp.abs(z)))
                if masked:
                    lk = jnp.where(before, lk, 0.0)
                hi = lk.astype(BF16)
                lo = (lk - hi.astype(F32)).astype(BF16)
                between = (jnp.dot(hi, upper[tk], preferred_element_type=F32)
                           + jnp.dot(lo, upper[tk], preferred_element_type=F32))
                r_old = run[...]
                w = jnp.exp((lk + z) + (between + r_old))
                if masked:
                    w = jnp.where(before, w, 0.0)
                run[...] = r_old + jnp.sum(lk, axis=1, keepdims=True)
                total = total + jnp.dot(w.astype(BF16), vh[hh], preferred_element_type=F32)
            acc[...] = total

        tile(qs, TQ, True)
        n_big = qs // SB_TK
        rem = qs - n_big * SB_TK
        @pl.when(rem > 0)
        def _():
            tile(pl.multiple_of(n_big * SB_TK, TQ), TQ, False)
        def big_body(j, c):
            tile(pl.multiple_of((n_big - 1 - j) * SB_TK, SB_TK), SB_TK, False)
            return c
        lax.fori_loop(0, n_big, big_body, 0)
        o_ref[pl.ds(qs, TQ), :] = acc[...].astype(o_ref.dtype)
        return carry

    lax.fori_loop(0, seq // TQ, q_body, 0)


def _sb_attention(qkv, batch, seq):
    P = SB_HEADS // 2
    kern = functools.partial(_sb_attn_kernel, seq=seq)
    blk = lambda off: pl.BlockSpec((seq, LANES), lambda b, p: (b, off + p))
    return pl.pallas_call(
        kern, grid=(batch, P),
        in_specs=[blk(0), blk(P), blk(2 * P)],
        out_specs=pl.BlockSpec((seq, LANES), lambda b, p: (b, p)),
        scratch_shapes=[pltpu.VMEM((TQ, LANES), F32), pltpu.VMEM((TQ, 1), F32),
                        pltpu.VMEM((TQ, 1), F32)],
        out_shape=jax.ShapeDtypeStruct((batch * seq, D_MODEL), BF16),
        compiler_params=_cparams(2), name="sb_attn",
    )(qkv, qkv, qkv)


def _proj_res_kernel(o_ref, w_ref, h_ref, out_ref):
    out_ref[...] = h_ref[...] + jnp.dot(o_ref[...], w_ref[...], preferred_element_type=F32)


def _proj_residual(o, w_bf16, h):
    n, d = h.shape
    row = lambda: pl.BlockSpec((ROW_TILE, d), lambda i: (i, 0))
    return pl.pallas_call(
        _proj_res_kernel, grid=(n // ROW_TILE,),
        in_specs=[row(), pl.BlockSpec((d, d), lambda i: (0, 0)), row()],
        out_specs=row(), out_shape=jax.ShapeDtypeStruct((n, d), F32),
        compiler_params=_cparams(1), name="proj_residual",
    )(o, w_bf16, h)


def _split3(x):
    hi = x.astype(BF16)
    return hi, (x - hi.astype(F32)).astype(BF16)


def _router_kernel(h_ref, g_ref, whi_ref, wlo_ref, b_ref, o_ref):
    xn = _rms(h_ref[...], g_ref[...])
    xhi, xlo = _split3(xn)
    logits = (jnp.dot(xhi, whi_ref[...], preferred_element_type=F32)
              + jnp.dot(xhi, wlo_ref[...], preferred_element_type=F32)
              + jnp.dot(xlo, whi_ref[...], preferred_element_type=F32)) + b_ref[...]
    rows = logits.shape[0]
    lane_i = lax.broadcasted_iota(jnp.int32, (rows, LANES), 1)
    lane = lane_i.astype(F32)
    ninf = -jnp.inf
    gl = jnp.where(lane_i < N_GROUPS, logits, ninf)
    gmax = jnp.max(gl, axis=1, keepdims=True)
    gidx = jnp.min(jnp.where(gl == gmax, lane, float(LANES)), axis=1, keepdims=True)
    gsum = jnp.sum(jnp.exp(gl - gmax), axis=1, keepdims=True)
    g_p = 1.0 / gsum
    e_lane = lane_i - N_GROUPS
    lane_group = jnp.right_shift(e_lane, EXPERT_SHIFT).astype(F32)
    in_group = (e_lane >= 0) & (e_lane < N_EXPERTS) & (lane_group == gidx)
    el = jnp.where(in_group, logits, ninf)
    v1 = jnp.max(el, axis=1, keepdims=True)
    i1 = jnp.min(jnp.where(el == v1, lane, float(LANES)), axis=1, keepdims=True)
    el2 = jnp.where(lane == i1, ninf, el)
    v2 = jnp.max(el2, axis=1, keepdims=True)
    i2 = jnp.min(jnp.where(el2 == v2, lane, float(LANES)), axis=1, keepdims=True)
    t = jnp.exp(v2 - v1)
    w1 = g_p / (1.0 + t)
    w2 = g_p * t / (1.0 + t)
    out = jnp.where(lane_i == 0, i1 - N_GROUPS,
          jnp.where(lane_i == 1, i2 - N_GROUPS,
          jnp.where(lane_i == 2, w1, jnp.where(lane_i == 3, w2, 0.0))))
    o_ref[...] = out


def _router(h, gain, w_hi, w_lo, bias):
    n, d = h.shape
    return pl.pallas_call(
        _router_kernel, grid=(n // ROW_TILE,),
        in_specs=[pl.BlockSpec((ROW_TILE, d), lambda i: (i, 0)),
                  pl.BlockSpec((1, d), lambda i: (0, 0)),
                  pl.BlockSpec((d, LANES), lambda i: (0, 0)),
                  pl.BlockSpec((d, LANES), lambda i: (0, 0)),
                  pl.BlockSpec((1, LANES), lambda i: (0, 0))],
        out_specs=pl.BlockSpec((ROW_TILE, LANES), lambda i: (i, 0)),
        out_shape=jax.ShapeDtypeStruct((n, LANES), F32),
        compiler_params=_cparams(1), name="moe_router",
    )(h, gain, w_hi, w_lo, bias)


def _gather_kernel(idx_hbm, src_hbm, o_ref, idx_smem, isem, sem):
    i = pl.program_id(0)
    cp = pltpu.make_async_copy(idx_hbm.at[i], idx_smem, isem)
    cp.start()
    cp.wait()
    def issue(r, c):
        pltpu.make_async_copy(src_hbm.at[pl.ds(idx_smem[r], 1)], o_ref.at[pl.ds(r, 1)], sem).start()
        return c
    lax.fori_loop(0, GATHER_ROWS, issue, 0)
    pltpu.make_async_copy(src_hbm.at[pl.ds(0, GATHER_ROWS)], o_ref, sem).wait()


def _gather_rows(idx2d, src):
    t, r = idx2d.shape
    d = src.shape[1]
    return pl.pallas_call(
        _gather_kernel, grid=(t,),
        in_specs=[pl.BlockSpec(memory_space=pl.ANY), pl.BlockSpec(memory_space=pl.ANY)],
        out_specs=pl.BlockSpec((r, d), lambda i: (i, 0)),
        scratch_shapes=[pltpu.SMEM((r,), jnp.int32), pltpu.SemaphoreType.DMA,
                        pltpu.SemaphoreType.DMA],
        out_shape=jax.ShapeDtypeStruct((t * r, d), src.dtype),
        compiler_params=_cparams(1), name="moe_gather",
    )(idx2d, src)


def _expert_kernel(te_ref, nv_ref, x_ref, g_ref, wg_ref, wu_ref, wd_ref, y_ref):
    t = pl.program_id(0)

    @pl.when(t < nv_ref[0])
    def _():
        xn = _rms(x_ref[...], g_ref[...]).astype(BF16)
        a = jnp.dot(xn, wg_ref[0], preferred_element_type=F32)
        u = jnp.dot(xn, wu_ref[0], preferred_element_type=F32)
        hid = (a / (1.0 + jnp.exp(-a))) * u
        y_ref[...] = jnp.dot(hid.astype(BF16), wd_ref[0], preferred_element_type=F32)

    @pl.when(t >= nv_ref[0])
    def _():
        y_ref[...] = jnp.zeros_like(y_ref)


def _expert_mlp(tile_expert, n_valid, xs, gain, wg, wu, wd):
    p, d = xs.shape
    f = wg.shape[2]
    tiles = p // EXPERT_TILE
    return pl.pallas_call(
        _expert_kernel,
        grid_spec=pltpu.PrefetchScalarGridSpec(
            num_scalar_prefetch=2, grid=(tiles,),
            in_specs=[pl.BlockSpec((EXPERT_TILE, d), lambda t, te, nv: (t, 0)),
                      pl.BlockSpec((1, d), lambda t, te, nv: (0, 0)),
                      pl.BlockSpec((1, d, f), lambda t, te, nv: (te[t], 0, 0)),
                      pl.BlockSpec((1, d, f), lambda t, te, nv: (te[t], 0, 0)),
                      pl.BlockSpec((1, f, d), lambda t, te, nv: (te[t], 0, 0))],
            out_specs=pl.BlockSpec((EXPERT_TILE, d), lambda t, te, nv: (t, 0))),
        out_shape=jax.ShapeDtypeStruct((p, d), F32),
        compiler_params=_cparams(1), name="moe_expert",
    )(tile_expert, n_valid, xs, gain, wg, wu, wd)


def _combine_kernel(pos_hbm, y_hbm, h_ref, r_ref, o_ref, buf, idx_smem, isem, sem):
    i = pl.program_id(0)
    R = GATHER_ROWS
    cp = pltpu.make_async_copy(pos_hbm.at[i], idx_smem, isem)
    cp.start()
    cp.wait()
    def issue(r, c):
        pltpu.make_async_copy(y_hbm.at[pl.ds(idx_smem[r], 1)], buf.at[pl.ds(r, 1)], sem).start()
        return c
    lax.fori_loop(0, 2 * R, issue, 0)
    pltpu.make_async_copy(y_hbm.at[pl.ds(0, 2 * R)], buf, sem).wait()
    route = r_ref[...]
    o_ref[...] = (h_ref[...] + route[:, 2:3] * buf[pl.ds(0, R), :]
                  + route[:, 3:4] * buf[pl.ds(R, R), :])


def _combine(pos2d, y, h, route):
    n, d = h.shape
    R = GATHER_ROWS
    return pl.pallas_call(
        _combine_kernel, grid=(n // R,),
        in_specs=[pl.BlockSpec(memory_space=pl.ANY), pl.BlockSpec(memory_space=pl.ANY),
                  pl.BlockSpec((R, d), lambda i: (i, 0)),
                  pl.BlockSpec((R, LANES), lambda i: (i, 0))],
        out_specs=pl.BlockSpec((R, d), lambda i: (i, 0)),
        scratch_shapes=[pltpu.VMEM((2 * R, d), F32), pltpu.SMEM((2 * R,), jnp.int32),
                        pltpu.SemaphoreType.DMA, pltpu.SemaphoreType.DMA],
        out_shape=jax.ShapeDtypeStruct((n, d), F32),
        compiler_params=_cparams(1), name="moe_combine",
    )(pos2d, y, h, route)


def _moe(h, gain, w_group, b_group, w_router, b_router, w_gate, w_up, w_down):
    n, d = h.shape
    pad = LANES - N_GROUPS - N_EXPERTS
    w_cat = jnp.concatenate([w_group, w_router, jnp.zeros((d, pad), F32)], axis=1)
    b_cat = jnp.concatenate([b_group, b_router, jnp.zeros((pad,), F32)])[None, :]
    w_hi = w_cat.astype(BF16)
    w_lo = (w_cat - w_hi.astype(F32)).astype(BF16)
    route = _router(h, gain, w_hi, w_lo, b_cat)

    e_flat = route[:, 0:2].astype(jnp.int32).reshape(-1)
    onehot = (e_flat[:, None] == jnp.arange(N_EXPERTS, dtype=jnp.int32)[None, :]).astype(jnp.int32)
    csum = jnp.cumsum(onehot, axis=0)
    counts = csum[-1]
    rank = jnp.sum((csum - onehot) * onehot, axis=1)
    ptiles = (counts + EXPERT_TILE - 1) // EXPERT_TILE
    tile_end = jnp.cumsum(ptiles)
    pstart = (tile_end - ptiles) * EXPERT_TILE
    pos = pstart[e_flat] + rank
    n_tiles = (2 * n) // EXPERT_TILE + N_EXPERTS
    n_slots = n_tiles * EXPERT_TILE
    slot_token = jnp.zeros((n_slots,), jnp.int32).at[pos].set(
        jnp.arange(2 * n, dtype=jnp.int32) // 2)
    tile_ids = jnp.arange(n_tiles, dtype=jnp.int32)
    tile_expert = jnp.minimum(
        jnp.sum((tile_ids[:, None] >= tile_end[None, :]).astype(jnp.int32), axis=1),
        N_EXPERTS - 1).astype(jnp.int32)
    n_valid = tile_end[-1:].astype(jnp.int32)
    last_expert = tile_expert[jnp.maximum(n_valid[0] - 1, 0)]
    tile_expert = jnp.where(tile_ids < n_valid[0], tile_expert, last_expert)

    xs = _gather_rows(slot_token.reshape(-1, GATHER_ROWS), h)
    y = _expert_mlp(tile_expert, n_valid, xs, gain,
                    w_gate.astype(BF16), w_up.astype(BF16), w_down.astype(BF16))
    pos2d = pos.reshape(n // GATHER_ROWS, GATHER_ROWS, 2).transpose(0, 2, 1).reshape(
        n // GATHER_ROWS, 2 * GATHER_ROWS)
    return _combine(pos2d, y, h, route)


def kernel(x, meta_tokens, norm_mix, norm_ffn, diff_w_qkv, diff_q_gain, diff_k_gain,
           diff_lambda_q1, diff_lambda_k1, diff_lambda_q2, diff_lambda_k2, diff_sub_gain,
           diff_w_o, sb_w_qkv, sb_w_o, moe_w_group, moe_b_group, moe_w_router,
           moe_b_router, moe_w_gate, moe_w_up, moe_w_down):
    B, S, D = x.shape
    L = N_META + S
    Lp = -(-L // Q_BLOCK) * Q_BLOCK
    depth = norm_mix.shape[0]
    assert D == D_MODEL and Lp % TQ == 0 and (B * Lp) % ROW_TILE == 0
    assert (B * Lp) % GATHER_ROWS == 0 and (2 * B * Lp) % EXPERT_TILE == 0

    meta = jnp.broadcast_to(meta_tokens.astype(x.dtype)[None], (B, N_META, D))
    h = jnp.concatenate([meta, x, jnp.zeros((B, Lp - L, D), x.dtype)], axis=1).reshape(B * Lp, D)

    blockdiag = (jnp.arange(MXU_DIM)[:, None] // DIFF_HEAD_DIM
                 == jnp.arange(MXU_DIM)[None, :] // DIFF_HEAD_DIM).astype(BF16)
    slopes = jnp.exp2(-8.0 * (jnp.arange(DIFF_HEADS, dtype=F32) + 1.0) / DIFF_HEADS)

    for i in range(depth):
        j = i // N_MIXERS
        gain = norm_mix[i][None, :]
        if i % N_MIXERS == 0:
            lambda_init = 0.8 - 0.6 * math.exp(-0.3 * i)
            scale = DIFF_HEAD_DIM ** -0.5
            reps = D // DIFF_HEAD_DIM
            qg = (jnp.tile(diff_q_gain[j], reps) * scale)[None, :]
            kg = jnp.tile(diff_k_gain[j], reps)[None, :]
            qkv = _norm_qkv(h, gain, diff_w_qkv[j].astype(BF16), (qg, kg, blockdiag))
            lam_params = jnp.stack([diff_lambda_q1[j], diff_lambda_k1[j],
                                    diff_lambda_q2[j], diff_lambda_k2[j]])
            o = _diff_attention(qkv, slopes, lam_params, diff_sub_gain[j][None, :], B, Lp,
                                lambda_init)
            h = _proj_residual(o, diff_w_o[j].astype(BF16), h)
        else:
            scale = SB_HEAD_DIM ** -0.5
            colscale = jnp.concatenate([jnp.full((D,), scale, F32), jnp.ones((2 * D,), F32)])
            w = (sb_w_qkv[j] * colscale[None, :]).astype(BF16)
            qkv = _norm_qkv(h, gain, w)
            o = _sb_attention(qkv, B, Lp)
            h = _proj_residual(o, sb_w_o[j].astype(BF16), h)
        h = _moe(h, norm_ffn[i][None, :], moe_w_group[i], moe_b_group[i], moe_w_router[i],
                 moe_b_router[i], moe_w_gate[i], moe_w_up[i], moe_w_down[i])
    return h.reshape(B, Lp, D)[:, N_META:N_META + S]
```

```python
import functools
import math

import jax
import jax.numpy as jnp
from jax import lax
from jax.experimental import pallas as pl
from jax.experimental.pallas import tpu as pltpu

F32 = jnp.float32
BF16 = jnp.bfloat16

D_MODEL = 1024
CHUNK = 64
CHUNK_SHIFT = 6
N_META = 16
Q_BLOCK = 128
N_MIXERS = 2
DIFF_HEADS = 8
DIFF_HEAD_DIM = D_MODEL // (2 * DIFF_HEADS)
SB_HEADS = 16
SB_HEAD_DIM = D_MODEL // SB_HEADS
N_GROUPS = 4
EXPERTS_PER_GROUP = 8
EXPERT_SHIFT = 3
N_EXPERTS = N_GROUPS * EXPERTS_PER_GROUP
D_EXPERT = 512
NORM_EPS = 1e-6

LANES = 128
MXU_DIM = 256
VMEM_LIMIT = 56 * 1024 * 1024

ROW_TILE = 512
TQ = 768
TK_BIG = 512
TK_MID = MXU_DIM
TK_SMALL = 128
ROW_BLOCK = 384
LOG2E = math.log2(math.e)
EXPERT_TILE = 256
GATHER_ROWS = 128
NEG_BIG = -1e30


def _cparams(n_axes):
    return pltpu.CompilerParams(
        dimension_semantics=("arbitrary",) * n_axes,
        vmem_limit_bytes=VMEM_LIMIT,
    )


def _rms(x, gain):
    ms = jnp.mean(x * x, axis=-1, keepdims=True)
    return x * lax.rsqrt(ms + NORM_EPS) * gain


def _norm_qkv_diff_kernel(h_ref, g_ref, w_ref, qg_ref, kg_ref, gm_ref, o_ref):
    D = D_MODEL
    xn = _rms(h_ref[...], g_ref[...]).astype(BF16)
    for part, gain_ref in ((0, qg_ref), (1, kg_ref)):
        y = jnp.dot(xn, w_ref[:, part * D:(part + 1) * D], preferred_element_type=F32)
        y2 = (y * y).astype(BF16)
        ss = jnp.concatenate(
            [jnp.dot(y2[:, c * MXU_DIM:(c + 1) * MXU_DIM], gm_ref[...],
                     preferred_element_type=F32) for c in range(D // MXU_DIM)], axis=1)
        yn = y * lax.rsqrt(ss * (1.0 / DIFF_HEAD_DIM) + NORM_EPS) * gain_ref[...]
        o_ref[:, part * D:(part + 1) * D] = yn.astype(BF16)
    v = jnp.dot(xn, w_ref[:, 2 * D:3 * D], preferred_element_type=F32)
    o_ref[:, 2 * D:3 * D] = v.astype(BF16)


def _norm_qkv_sb_kernel(h_ref, g_ref, w_ref, o_ref):
    D = D_MODEL
    xn = _rms(h_ref[...], g_ref[...]).astype(BF16)
    for part in range(3):
        y = jnp.dot(xn, w_ref[:, part * D:(part + 1) * D], preferred_element_type=F32)
        o_ref[:, part * D:(part + 1) * D] = y.astype(BF16)


def _norm_qkv(h, gain, w_bf16, qk=None):
    n, d = h.shape
    grid = (n // ROW_TILE,)
    row = pl.BlockSpec((ROW_TILE, d), lambda i: (i, 0))
    vec = pl.BlockSpec((1, d), lambda i: (0, 0))
    wspec = pl.BlockSpec((d, 3 * d), lambda i: (0, 0))
    out = pl.BlockSpec((ROW_TILE, 3 * d), lambda i: (i, 0))
    out_shape = jax.ShapeDtypeStruct((n, 3 * d), BF16)
    if qk is None:
        return pl.pallas_call(
            _norm_qkv_sb_kernel, grid=grid, in_specs=[row, vec, wspec], out_specs=out,
            out_shape=out_shape, compiler_params=_cparams(1), name="norm_qkv_sb",
        )(h, gain, w_bf16)
    qg, kg, gm = qk
    gspec = pl.BlockSpec((MXU_DIM, MXU_DIM), lambda i: (0, 0))
    return pl.pallas_call(
        _norm_qkv_diff_kernel, grid=grid, in_specs=[row, vec, wspec, vec, vec, gspec],
        out_specs=out, out_shape=out_shape, compiler_params=_cparams(1), name="norm_qkv_diff",
    )(h, gain, w_bf16, qg, kg, gm)


def _diff_attn_kernel(slopes_ref, q_ref, k_ref, v_ref, lam_ref, sg_ref, o_ref,
                      acc0, acc1, m0, m1, *, seq, lambda_init):
    hd = pl.program_id(1)
    slope = slopes_ref[hd] * LOG2E
    accs, ms = (acc0, acc1), (m0, m1)
    lp = lam_ref[...]
    lam = (jnp.exp(jnp.sum(lp[0:1] * lp[1:2], axis=-1, keepdims=True))
           - jnp.exp(jnp.sum(lp[2:3] * lp[3:4], axis=-1, keepdims=True)) + lambda_init)
    ones = jnp.ones((TK_BIG, LANES), BF16)
    lane = lax.broadcasted_iota(jnp.int32, (TQ, LANES), 1)
    dn = (((1,), (1,)), ((), ()))

    def q_body(qi, carry):
        qs = pl.multiple_of(qi * TQ, TQ)
        q = q_ref[pl.ds(qs, TQ), :]
        zero = jnp.zeros_like(q)
        qc = (jnp.where(lane < DIFF_HEAD_DIM, q, zero), jnp.where(lane >= DIFF_HEAD_DIM, q, zero))
        qpos = qs + lax.broadcasted_iota(jnp.int32, (TQ, 1), 0)
        bound = ((qpos + (CHUNK - N_META)) >> CHUNK_SHIFT) * CHUNK + N_META
        for acc, m in ((acc0, m0), (acc1, m1)):
            acc[...] = jnp.zeros_like(acc)
            m[...] = jnp.full_like(m, NEG_BIG)

        def tile(kstart, tk, masked, row0=0):
            k = k_ref[pl.ds(kstart, tk), :]
            v = v_ref[pl.ds(kstart, tk), :]
            vext = jnp.concatenate([v, ones[:tk]], axis=1)
            kpos = kstart + lax.broadcasted_iota(jnp.int32, (1, tk), 1)
            rel = (kpos - qs).astype(F32) * slope
            for r0 in range(row0, TQ, ROW_BLOCK):
                r1 = min(r0 + ROW_BLOCK, TQ)
                rows = pl.ds(r0, r1 - r0)
                if masked:
                    ahead = jnp.maximum(kpos - qpos[r0:r1], 0).astype(F32) * (2.0 * slope)
                    allowed = kpos < bound[r0:r1]
                for c in range(2):
                    acc, m = accs[c], ms[c]
                    s = lax.dot_general(qc[c][r0:r1], k, dn, preferred_element_type=F32) + rel
                    if masked:
                        s = jnp.where(allowed, s - ahead, -jnp.inf)
                    m_old = m[rows, :]
                    m_new = jnp.maximum(m_old, jnp.max(s, axis=1, keepdims=True))
                    alpha = jnp.exp2(m_old - m_new)
                    p = jnp.exp2(s - m_new).astype(BF16)
                    acc[rows, :] = (acc[rows, :] * alpha
                                    + jnp.dot(p, vext, preferred_element_type=F32))
                    m[rows, :] = m_new

        n_big = qs // TK_BIG
        def big_body(j, c):
            tile(pl.multiple_of(j * TK_BIG, TK_BIG), TK_BIG, False)
            return c
        lax.fori_loop(0, n_big, big_body, 0)
        base = pl.multiple_of(n_big * TK_BIG, TK_BIG)
        @pl.when(qs > base)
        def _():
            tile(base, TK_MID, False)
        for c in range(TQ // TK_MID):
            tile(pl.multiple_of(qs + c * TK_MID, TK_MID), TK_MID, True, max(0, c * TK_MID - CHUNK))
        @pl.when(qs + TQ < seq)
        def _():
            tile(pl.multiple_of(qs + TQ, TK_SMALL), TK_SMALL, True, TQ - CHUNK)

        a0 = acc0[...]
        a1 = acc1[...]
        o = a0[:, :LANES] / a0[:, LANES:] - lam * (a1[:, :LANES] / a1[:, LANES:])
        o = _rms(o, sg_ref[...]) * (1.0 - lambda_init)
        o_ref[pl.ds(qs, TQ), :] = o.astype(o_ref.dtype)
        return carry

    lax.fori_loop(0, seq // TQ, q_body, 0)


def _diff_attention(qkv, slopes, lam_params, sub_gain, batch, seq, lambda_init):
    H = DIFF_HEADS
    kern = functools.partial(_diff_attn_kernel, seq=seq, lambda_init=lambda_init)
    blk = lambda off: pl.BlockSpec((seq, LANES), lambda b, h, s: (b, off + h))
    return pl.pallas_call(
        kern,
        grid_spec=pltpu.PrefetchScalarGridSpec(
            num_scalar_prefetch=1, grid=(batch, H),
            in_specs=[blk(0), blk(H), blk(2 * H),
                      pl.BlockSpec((4, DIFF_HEAD_DIM), lambda b, h, s: (0, 0)),
                      pl.BlockSpec((1, LANES), lambda b, h, s: (0, 0))],
            out_specs=pl.BlockSpec((seq, LANES), lambda b, h, s: (b, h)),
            scratch_shapes=[pltpu.VMEM((TQ, 2 * LANES), F32), pltpu.VMEM((TQ, 2 * LANES), F32),
                            pltpu.VMEM((TQ, 1), F32), pltpu.VMEM((TQ, 1), F32)]),
        out_shape=jax.ShapeDtypeStruct((batch * seq, D_MODEL), BF16),
        compiler_params=_cparams(2), name="diff_attn",
    )(slopes, qkv, qkv, qkv, lam_params, sub_gain)


def _sb_attn_kernel(q_ref, k_ref, v_ref, o_ref, acc, run0, run1, *, seq):
    d = SB_HEAD_DIM
    lane = lax.broadcasted_iota(jnp.int32, (TQ, LANES), 1)
    dn = (((1,), (1,)), ((), ()))

    tk = TK_MID
    r_i = lax.broadcasted_iota(jnp.int32, (tk, tk), 0)
    c_i = lax.broadcasted_iota(jnp.int32, (tk, tk), 1)
    later = jnp.where(r_i > c_i, 1.0, 0.0).astype(BF16)
    vlow = lax.broadcasted_iota(jnp.int32, (tk, LANES), 1) < d
    runs = (run0, run1)

    def q_body(qi, carry):
        qs = pl.multiple_of(qi * TQ, TQ)
        q = q_ref[pl.ds(qs, TQ), :]
        zero = jnp.zeros_like(q)
        qh = (jnp.where(lane < d, q, zero), jnp.where(lane >= d, q, zero))
        qpos = qs + lax.broadcasted_iota(jnp.int32, (TQ, 1), 0)
        acc[...] = jnp.zeros_like(acc)
        run0[...] = jnp.zeros_like(run0)
        run1[...] = jnp.zeros_like(run1)

        def tile(kstart, masked, row0=0):
            k = k_ref[pl.ds(kstart, tk), :]
            v = v_ref[pl.ds(kstart, tk), :]
            vzero = jnp.zeros_like(v)
            vh = (jnp.where(vlow, v, vzero), jnp.where(vlow, vzero, v))
            kpos = kstart + lax.broadcasted_iota(jnp.int32, (1, tk), 1)
            for r0 in range(row0, TQ, ROW_BLOCK):
                r1 = min(r0 + ROW_BLOCK, TQ)
                rows = pl.ds(r0, r1 - r0)
                if masked:
                    before = kpos < qpos[r0:r1]
                total = acc[rows, :]
                for hh in range(2):
                    run = runs[hh]
                    z = lax.dot_general(qh[hh][r0:r1], k, dn, preferred_element_type=F32)
                    drop = jnp.maximum(z, 0.0) + jnp.log2(1.0 + jnp.exp2(-jnp.abs(z)))
                    if masked:
                        drop = jnp.where(before, drop, 0.0)
                    later_drop = jnp.dot(drop.astype(BF16), later, preferred_element_type=F32)
                    r_old = run[rows, :]
                    w = jnp.exp2((z - drop) - (later_drop + r_old))
                    if masked:
                        w = jnp.where(before, w, 0.0)
                    run[rows, :] = r_old + jnp.sum(drop, axis=1, keepdims=True)
                    total = total + jnp.dot(w.astype(BF16), vh[hh], preferred_element_type=F32)
                acc[rows, :] = total

        for c in reversed(range(TQ // tk)):
            tile(pl.multiple_of(qs + c * tk, tk), True, c * tk)
        n_full = qs // tk
        def full_body(j, c):
            tile(pl.multiple_of((n_full - 1 - j) * tk, tk), False)
            return c
        lax.fori_loop(0, n_full, full_body, 0)
        o_ref[pl.ds(qs, TQ), :] = acc[...].astype(o_ref.dtype)
        return carry

    lax.fori_loop(0, seq // TQ, q_body, 0)


def _sb_attention(qkv, batch, seq):
    P = SB_HEADS // 2
    kern = functools.partial(_sb_attn_kernel, seq=seq)
    blk = lambda off: pl.BlockSpec((seq, LANES), lambda b, p: (b, off + p))
    return pl.pallas_call(
        kern, grid=(batch, P),
        in_specs=[blk(0), blk(P), blk(2 * P)],
        out_specs=pl.BlockSpec((seq, LANES), lambda b, p: (b, p)),
        scratch_shapes=[pltpu.VMEM((TQ, LANES), F32), pltpu.VMEM((TQ, 1), F32),
                        pltpu.VMEM((TQ, 1), F32)],
        out_shape=jax.ShapeDtypeStruct((batch * seq, D_MODEL), BF16),
        compiler_params=_cparams(2), name="sb_attn",
    )(qkv, qkv, qkv)


def _proj_res_kernel(o_ref, w_ref, h_ref, out_ref):
    out_ref[...] = h_ref[...] + jnp.dot(o_ref[...], w_ref[...], preferred_element_type=F32)


def _proj_residual(o, w_bf16, h):
    n, d = h.shape
    row = lambda: pl.BlockSpec((ROW_TILE, d), lambda i: (i, 0))
    return pl.pallas_call(
        _proj_res_kernel, grid=(n // ROW_TILE,),
        in_specs=[row(), pl.BlockSpec((d, d), lambda i: (0, 0)), row()],
        out_specs=row(), out_shape=jax.ShapeDtypeStruct((n, d), F32),
        compiler_params=_cparams(1), name="proj_residual",
    )(o, w_bf16, h)


def _split3(x):
    hi = x.astype(BF16)
    return hi, (x - hi.astype(F32)).astype(BF16)


def _router_kernel(h_ref, g_ref, whi_ref, wlo_ref, b_ref, o_ref):
    xn = _rms(h_ref[...], g_ref[...])
    xhi, xlo = _split3(xn)
    logits = (jnp.dot(xhi, whi_ref[...], preferred_element_type=F32)
              + jnp.dot(xhi, wlo_ref[...], preferred_element_type=F32)
              + jnp.dot(xlo, whi_ref[...], preferred_element_type=F32)) + b_ref[...]
    rows = logits.shape[0]
    lane_i = lax.broadcasted_iota(jnp.int32, (rows, LANES), 1)
    lane = lane_i.astype(F32)
    ninf = -jnp.inf
    gl = jnp.where(lane_i < N_GROUPS, logits, ninf)
    gmax = jnp.max(gl, axis=1, keepdims=True)
    gidx = jnp.min(jnp.where(gl == gmax, lane, float(LANES)), axis=1, keepdims=True)
    gsum = jnp.sum(jnp.exp(gl - gmax), axis=1, keepdims=True)
    g_p = 1.0 / gsum
    e_lane = lane_i - N_GROUPS
    lane_group = jnp.right_shift(e_lane, EXPERT_SHIFT).astype(F32)
    in_group = (e_lane >= 0) & (e_lane < N_EXPERTS) & (lane_group == gidx)
    el = jnp.where(in_group, logits, ninf)
    v1 = jnp.max(el, axis=1, keepdims=True)
    i1 = jnp.min(jnp.where(el == v1, lane, float(LANES)), axis=1, keepdims=True)
    el2 = jnp.where(lane == i1, ninf, el)
    v2 = jnp.max(el2, axis=1, keepdims=True)
    i2 = jnp.min(jnp.where(el2 == v2, lane, float(LANES)), axis=1, keepdims=True)
    t = jnp.exp(v2 - v1)
    w1 = g_p / (1.0 + t)
    w2 = g_p * t / (1.0 + t)
    out = jnp.where(lane_i == 0, i1 - N_GROUPS,
          jnp.where(lane_i == 1, i2 - N_GROUPS,
          jnp.where(lane_i == 2, w1, jnp.where(lane_i == 3, w2, 0.0))))
    o_ref[...] = out


def _router(h, gain, w_hi, w_lo, bias):
    n, d = h.shape
    return pl.pallas_call(
        _router_kernel, grid=(n // ROW_TILE,),
        in_specs=[pl.BlockSpec((ROW_TILE, d), lambda i: (i, 0)),
                  pl.BlockSpec((1, d), lambda i: (0, 0)),
                  pl.BlockSpec((d, LANES), lambda i: (0, 0)),
                  pl.BlockSpec((d, LANES), lambda i: (0, 0)),
                  pl.BlockSpec((1, LANES), lambda i: (0, 0))],
        out_specs=pl.BlockSpec((ROW_TILE, LANES), lambda i: (i, 0)),
        out_shape=jax.ShapeDtypeStruct((n, LANES), F32),
        compiler_params=_cparams(1), name="moe_router",
    )(h, gain, w_hi, w_lo, bias)


def _gather_kernel(idx_hbm, src_hbm, o_ref, idx_smem, isem, sem):
    i = pl.program_id(0)
    cp = pltpu.make_async_copy(idx_hbm.at[i], idx_smem, isem)
    cp.start()
    cp.wait()
    def issue(r, c):
        pltpu.make_async_copy(src_hbm.at[pl.ds(idx_smem[r], 1)], o_ref.at[pl.ds(r, 1)], sem).start()
        return c
    lax.fori_loop(0, GATHER_ROWS, issue, 0)
    pltpu.make_async_copy(src_hbm.at[pl.ds(0, GATHER_ROWS)], o_ref, sem).wait()


def _gather_rows(idx2d, src):
    t, r = idx2d.shape
    d = src.shape[1]
    return pl.pallas_call(
        _gather_kernel, grid=(t,),
        in_specs=[pl.BlockSpec(memory_space=pl.ANY), pl.BlockSpec(memory_space=pl.ANY)],
        out_specs=pl.BlockSpec((r, d), lambda i: (i, 0)),
        scratch_shapes=[pltpu.SMEM((r,), jnp.int32), pltpu.SemaphoreType.DMA,
                        pltpu.SemaphoreType.DMA],
        out_shape=jax.ShapeDtypeStruct((t * r, d), src.dtype),
        compiler_params=_cparams(1), name="moe_gather",
    )(idx2d, src)


def _expert_kernel(te_ref, nv_ref, x_ref, g_ref, wg_ref, wu_ref, wd_ref, y_ref):
    t = pl.program_id(0)

    @pl.when(t < nv_ref[0])
    def _():
        xn = _rms(x_ref[...], g_ref[...]).astype(BF16)
        a = jnp.dot(xn, wg_ref[0], preferred_element_type=F32)
        u = jnp.dot(xn, wu_ref[0], preferred_element_type=F32)
        hid = (a / (1.0 + jnp.exp(-a))) * u
        y_ref[...] = jnp.dot(hid.astype(BF16), wd_ref[0], preferred_element_type=F32)

    @pl.when(t >= nv_ref[0])
    def _():
        y_ref[...] = jnp.zeros_like(y_ref)


def _expert_mlp(tile_expert, n_valid, xs, gain, wg, wu, wd):
    p, d = xs.shape
    f = wg.shape[2]
    tiles = p // EXPERT_TILE
    return pl.pallas_call(
        _expert_kernel,
        grid_spec=pltpu.PrefetchScalarGridSpec(
            num_scalar_prefetch=2, grid=(tiles,),
            in_specs=[pl.BlockSpec((EXPERT_TILE, d), lambda t, te, nv: (t, 0)),
                      pl.BlockSpec((1, d), lambda t, te, nv: (0, 0)),
                      pl.BlockSpec((1, d, f), lambda t, te, nv: (te[t], 0, 0)),
                      pl.BlockSpec((1, d, f), lambda t, te, nv: (te[t], 0, 0)),
                      pl.BlockSpec((1, f, d), lambda t, te, nv: (te[t], 0, 0))],
            out_specs=pl.BlockSpec((EXPERT_TILE, d), lambda t, te, nv: (t, 0))),
        out_shape=jax.ShapeDtypeStruct((p, d), F32),
        compiler_params=_cparams(1), name="moe_expert",
    )(tile_expert, n_valid, xs, gain, wg, wu, wd)


def _combine_kernel(pos_hbm, y_hbm, h_ref, r_ref, o_ref, buf, idx_smem, isem, sem):
    i = pl.program_id(0)
    R = GATHER_ROWS
    cp = pltpu.make_async_copy(pos_hbm.at[i], idx_smem, isem)
    cp.start()
    cp.wait()
    def issue(r, c):
        pltpu.make_async_copy(y_hbm.at[pl.ds(idx_smem[r], 1)], buf.at[pl.ds(r, 1)], sem).start()
        return c
    lax.fori_loop(0, 2 * R, issue, 0)
    pltpu.make_async_copy(y_hbm.at[pl.ds(0, 2 * R)], buf, sem).wait()
    route = r_ref[...]
    o_ref[...] = (h_ref[...] + route[:, 2:3] * buf[pl.ds(0, R), :]
                  + route[:, 3:4] * buf[pl.ds(R, R), :])


def _combine(pos2d, y, h, route):
    n, d = h.shape
    R = GATHER_ROWS
    return pl.pallas_call(
        _combine_kernel, grid=(n // R,),
        in_specs=[pl.BlockSpec(memory_space=pl.ANY), pl.BlockSpec(memory_space=pl.ANY),
                  pl.BlockSpec((R, d), lambda i: (i, 0)),
                  pl.BlockSpec((R, LANES), lambda i: (i, 0))],
        out_specs=pl.BlockSpec((R, d), lambda i: (i, 0)),
        scratch_shapes=[pltpu.VMEM((2 * R, d), F32), pltpu.SMEM((2 * R,), jnp.int32),
                        pltpu.SemaphoreType.DMA, pltpu.SemaphoreType.DMA],
        out_shape=jax.ShapeDtypeStruct((n, d), F32),
        compiler_params=_cparams(1), name="moe_combine",
    )(pos2d, y, h, route)


def _moe(h, gain, w_group, b_group, w_router, b_router, w_gate, w_up, w_down):
    n, d = h.shape
    pad = LANES - N_GROUPS - N_EXPERTS
    w_cat = jnp.concatenate([w_group, w_router, jnp.zeros((d, pad), F32)], axis=1)
    b_cat = jnp.concatenate([b_group, b_router, jnp.zeros((pad,), F32)])[None, :]
    w_hi = w_cat.astype(BF16)
    w_lo = (w_cat - w_hi.astype(F32)).astype(BF16)
    route = _router(h, gain, w_hi, w_lo, b_cat)

    e_flat = route[:, 0:2].astype(jnp.int32).reshape(-1)
    onehot = (e_flat[:, None] == jnp.arange(N_EXPERTS, dtype=jnp.int32)[None, :]).astype(jnp.int32)
    csum = jnp.cumsum(onehot, axis=0)
    counts = csum[-1]
    rank = jnp.sum((csum - onehot) * onehot, axis=1)
    ptiles = (counts + EXPERT_TILE - 1) // EXPERT_TILE
    tile_end = jnp.cumsum(ptiles)
    pstart = (tile_end - ptiles) * EXPERT_TILE
    pos = pstart[e_flat] + rank
    n_tiles = (2 * n) // EXPERT_TILE + N_EXPERTS
    n_slots = n_tiles * EXPERT_TILE
    slot_token = jnp.zeros((n_slots,), jnp.int32).at[pos].set(
        jnp.arange(2 * n, dtype=jnp.int32) // 2)
    tile_ids = jnp.arange(n_tiles, dtype=jnp.int32)
    tile_expert = jnp.minimum(
        jnp.sum((tile_ids[:, None] >= tile_end[None, :]).astype(jnp.int32), axis=1),
        N_EXPERTS - 1).astype(jnp.int32)
    n_valid = tile_end[-1:].astype(jnp.int32)
    last_expert = tile_expert[jnp.maximum(n_valid[0] - 1, 0)]
    tile_expert = jnp.where(tile_ids < n_valid[0], tile_expert, last_expert)

    xs = _gather_rows(slot_token.reshape(-1, GATHER_ROWS), h)
    y = _expert_mlp(tile_expert, n_valid, xs, gain,
                    w_gate.astype(BF16), w_up.astype(BF16), w_down.astype(BF16))
    pos2d = pos.reshape(n // GATHER_ROWS, GATHER_ROWS, 2).transpose(0, 2, 1).reshape(
        n // GATHER_ROWS, 2 * GATHER_ROWS)
    return _combine(pos2d, y, h, route)


def kernel(x, meta_tokens, norm_mix, norm_ffn, diff_w_qkv, diff_q_gain, diff_k_gain,
           diff_lambda_q1, diff_lambda_k1, diff_lambda_q2, diff_lambda_k2, diff_sub_gain,
           diff_w_o, sb_w_qkv, sb_w_o, moe_w_group, moe_b_group, moe_w_router,
           moe_b_router, moe_w_gate, moe_w_up, moe_w_down):
    B, S, D = x.shape
    L = N_META + S
    Lp = -(-L // TQ) * TQ
    depth = norm_mix.shape[0]
    assert D == D_MODEL and TQ % Q_BLOCK == 0 and TQ % TK_MID == 0 and TK_BIG == 2 * TK_MID
    assert (B * Lp) % ROW_TILE == 0
    assert (B * Lp) % GATHER_ROWS == 0 and (2 * B * Lp) % EXPERT_TILE == 0

    meta = jnp.broadcast_to(meta_tokens.astype(x.dtype)[None], (B, N_META, D))
    h = jnp.concatenate([meta, x, jnp.zeros((B, Lp - L, D), x.dtype)], axis=1).reshape(B * Lp, D)

    blockdiag = (jnp.arange(MXU_DIM)[:, None] // DIFF_HEAD_DIM
                 == jnp.arange(MXU_DIM)[None, :] // DIFF_HEAD_DIM).astype(BF16)
    slopes = jnp.exp2(-8.0 * (jnp.arange(DIFF_HEADS, dtype=F32) + 1.0) / DIFF_HEADS)

    for i in range(depth):
        j = i // N_MIXERS
        gain = norm_mix[i][None, :]
        if i % N_MIXERS == 0:
            lambda_init = 0.8 - 0.6 * math.exp(-0.3 * i)
            scale = DIFF_HEAD_DIM ** -0.5 * LOG2E
            reps = D // DIFF_HEAD_DIM
            qg = (jnp.tile(diff_q_gain[j], reps) * scale)[None, :]
            kg = jnp.tile(diff_k_gain[j], reps)[None, :]
            qkv = _norm_qkv(h, gain, diff_w_qkv[j].astype(BF16), (qg, kg, blockdiag))
            lam_params = jnp.stack([diff_lambda_q1[j], diff_lambda_k1[j],
                                    diff_lambda_q2[j], diff_lambda_k2[j]])
            o = _diff_attention(qkv, slopes, lam_params, diff_sub_gain[j][None, :], B, Lp,
                                lambda_init)
            h = _proj_residual(o, diff_w_o[j].astype(BF16), h)
        else:
            scale = SB_HEAD_DIM ** -0.5 * LOG2E
            colscale = jnp.concatenate([jnp.full((D,), scale, F32), jnp.ones((2 * D,), F32)])
            w = (sb_w_qkv[j] * colscale[None, :]).astype(BF16)
            qkv = _norm_qkv(h, gain, w)
            o = _sb_attention(qkv, B, Lp)
            h = _proj_residual(o, sb_w_o[j].astype(BF16), h)
        h = _moe(h, norm_ffn[i][None, :], moe_w_group[i], moe_b_group[i], moe_w_router[i],
                 moe_b_router[i], moe_w_gate[i], moe_w_up[i], moe_w_down[i])
    return h.reshape(B, Lp, D)[:, N_META:N_META + S]
```

```python
import functools
import math

import jax
import jax.numpy as jnp
from jax import lax
from jax.experimental import pallas as pl
from jax.experimental.pallas import tpu as pltpu

F32 = jnp.float32
BF16 = jnp.bfloat16

D_MODEL = 1024
CHUNK = 64
CHUNK_SHIFT = 6
N_META = 16
Q_BLOCK = 128
N_MIXERS = 2
DIFF_HEADS = 8
DIFF_HEAD_DIM = D_MODEL // (2 * DIFF_HEADS)
SB_HEADS = 16
SB_HEAD_DIM = D_MODEL // SB_HEADS
N_GROUPS = 4
EXPERTS_PER_GROUP = 8
EXPERT_SHIFT = 3
N_EXPERTS = N_GROUPS * EXPERTS_PER_GROUP
D_EXPERT = 512
NORM_EPS = 1e-6

LANES = 128
SUBLANES = 8
SUBLANE_SHIFT = 3
MXU_DIM = 256
VMEM_LIMIT = 56 * 1024 * 1024

ROW_TILE = 512
TQ = 768
TK_BIG = 512
TK_MID = MXU_DIM
TK_SMALL = 128
ROW_BLOCK = 384
LOG2E = math.log2(math.e)
EXPERT_TILE = 256
DMA_UNROLL = 8
NEG_BIG = -1e30


def _cparams(n_axes):
    return pltpu.CompilerParams(
        dimension_semantics=("arbitrary",) * n_axes,
        vmem_limit_bytes=VMEM_LIMIT,
    )


def _rms(x, gain):
    ms = jnp.mean(x * x, axis=-1, keepdims=True)
    return x * lax.rsqrt(ms + NORM_EPS) * gain


def _norm_qkv_diff_kernel(h_ref, g_ref, w_ref, qg_ref, kg_ref, gm_ref, o_ref):
    D = D_MODEL
    xn = _rms(h_ref[...], g_ref[...]).astype(BF16)
    for part, gain_ref in ((0, qg_ref), (1, kg_ref)):
        y = jnp.dot(xn, w_ref[:, part * D:(part + 1) * D], preferred_element_type=F32)
        y2 = (y * y).astype(BF16)
        ss = jnp.concatenate(
            [jnp.dot(y2[:, c * MXU_DIM:(c + 1) * MXU_DIM], gm_ref[...],
                     preferred_element_type=F32) for c in range(D // MXU_DIM)], axis=1)
        yn = y * lax.rsqrt(ss * (1.0 / DIFF_HEAD_DIM) + NORM_EPS) * gain_ref[...]
        o_ref[:, part * D:(part + 1) * D] = yn.astype(BF16)
    v = jnp.dot(xn, w_ref[:, 2 * D:3 * D], preferred_element_type=F32)
    o_ref[:, 2 * D:3 * D] = v.astype(BF16)


def _norm_qkv_sb_kernel(h_ref, g_ref, w_ref, o_ref):
    D = D_MODEL
    xn = _rms(h_ref[...], g_ref[...]).astype(BF16)
    for part in range(3):
        y = jnp.dot(xn, w_ref[:, part * D:(part + 1) * D], preferred_element_type=F32)
        o_ref[:, part * D:(part + 1) * D] = y.astype(BF16)


def _norm_qkv(h, gain, w_bf16, qk=None):
    n, d = h.shape
    grid = (n // ROW_TILE,)
    row = pl.BlockSpec((ROW_TILE, d), lambda i: (i, 0))
    vec = pl.BlockSpec((1, d), lambda i: (0, 0))
    wspec = pl.BlockSpec((d, 3 * d), lambda i: (0, 0))
    out = pl.BlockSpec((ROW_TILE, 3 * d), lambda i: (i, 0))
    out_shape = jax.ShapeDtypeStruct((n, 3 * d), BF16)
    if qk is None:
        return pl.pallas_call(
            _norm_qkv_sb_kernel, grid=grid, in_specs=[row, vec, wspec], out_specs=out,
            out_shape=out_shape, compiler_params=_cparams(1), name="norm_qkv_sb",
        )(h, gain, w_bf16)
    qg, kg, gm = qk
    gspec = pl.BlockSpec((MXU_DIM, MXU_DIM), lambda i: (0, 0))
    return pl.pallas_call(
        _norm_qkv_diff_kernel, grid=grid, in_specs=[row, vec, wspec, vec, vec, gspec],
        out_specs=out, out_shape=out_shape, compiler_params=_cparams(1), name="norm_qkv_diff",
    )(h, gain, w_bf16, qg, kg, gm)


def _diff_attn_kernel(slopes_ref, q_ref, k_ref, v_ref, lam_ref, sg_ref, o_ref,
                      acc0, acc1, m0, m1, *, seq, lambda_init):
    hd = pl.program_id(1)
    slope = slopes_ref[hd] * LOG2E
    accs, ms = (acc0, acc1), (m0, m1)
    lp = lam_ref[...]
    lam = (jnp.exp(jnp.sum(lp[0:1] * lp[1:2], axis=-1, keepdims=True))
           - jnp.exp(jnp.sum(lp[2:3] * lp[3:4], axis=-1, keepdims=True)) + lambda_init)
    ones = jnp.ones((TK_BIG, LANES), BF16)
    lane = lax.broadcasted_iota(jnp.int32, (TQ, LANES), 1)
    dn = (((1,), (1,)), ((), ()))

    def q_body(qi, carry):
        qs = pl.multiple_of(qi * TQ, TQ)
        q = q_ref[pl.ds(qs, TQ), :]
        zero = jnp.zeros_like(q)
        qc = (jnp.where(lane < DIFF_HEAD_DIM, q, zero), jnp.where(lane >= DIFF_HEAD_DIM, q, zero))
        qpos = qs + lax.broadcasted_iota(jnp.int32, (TQ, 1), 0)
        bound = ((qpos + (CHUNK - N_META)) >> CHUNK_SHIFT) * CHUNK + N_META
        for acc, m in ((acc0, m0), (acc1, m1)):
            acc[...] = jnp.zeros_like(acc)
            m[...] = jnp.full_like(m, NEG_BIG)

        def tile(kstart, tk, masked, row0=0):
            k = k_ref[pl.ds(kstart, tk), :]
            v = v_ref[pl.ds(kstart, tk), :]
            vext = jnp.concatenate([v, ones[:tk]], axis=1)
            kpos = kstart + lax.broadcasted_iota(jnp.int32, (1, tk), 1)
            rel = (kpos - qs).astype(F32) * slope
            for r0 in range(row0, TQ, ROW_BLOCK):
                r1 = min(r0 + ROW_BLOCK, TQ)
                rows = pl.ds(r0, r1 - r0)
                if masked:
                    ahead = jnp.maximum(kpos - qpos[r0:r1], 0).astype(F32) * (2.0 * slope)
                    allowed = kpos < bound[r0:r1]
                for c in range(2):
                    acc, m = accs[c], ms[c]
                    s = lax.dot_general(qc[c][r0:r1], k, dn, preferred_element_type=F32) + rel
                    if masked:
                        s = jnp.where(allowed, s - ahead, -jnp.inf)
                    m_old = m[rows, :]
                    m_new = jnp.maximum(m_old, jnp.max(s, axis=1, keepdims=True))
                    alpha = jnp.exp2(m_old - m_new)
                    p = jnp.exp2(s - m_new).astype(BF16)
                    acc[rows, :] = (acc[rows, :] * alpha
                                    + jnp.dot(p, vext, preferred_element_type=F32))
                    m[rows, :] = m_new

        n_big = qs // TK_BIG
        def big_body(j, c):
            tile(pl.multiple_of(j * TK_BIG, TK_BIG), TK_BIG, False)
            return c
        lax.fori_loop(0, n_big, big_body, 0)
        base = pl.multiple_of(n_big * TK_BIG, TK_BIG)
        @pl.when(qs > base)
        def _():
            tile(base, TK_MID, False)
        for c in range(TQ // TK_MID):
            tile(pl.multiple_of(qs + c * TK_MID, TK_MID), TK_MID, True, max(0, c * TK_MID - CHUNK))
        @pl.when(qs + TQ < seq)
        def _():
            tile(pl.multiple_of(qs + TQ, TK_SMALL), TK_SMALL, True, TQ - CHUNK)

        a0 = acc0[...]
        a1 = acc1[...]
        o = a0[:, :LANES] / a0[:, LANES:] - lam * (a1[:, :LANES] / a1[:, LANES:])
        o = _rms(o, sg_ref[...]) * (1.0 - lambda_init)
        o_ref[pl.ds(qs, TQ), :] = o.astype(o_ref.dtype)
        return carry

    lax.fori_loop(0, seq // TQ, q_body, 0)


def _diff_attention(qkv, slopes, lam_params, sub_gain, batch, seq, lambda_init):
    H = DIFF_HEADS
    kern = functools.partial(_diff_attn_kernel, seq=seq, lambda_init=lambda_init)
    blk = lambda off: pl.BlockSpec((seq, LANES), lambda b, h, s: (b, off + h))
    return pl.pallas_call(
        kern,
        grid_spec=pltpu.PrefetchScalarGridSpec(
            num_scalar_prefetch=1, grid=(batch, H),
            in_specs=[blk(0), blk(H), blk(2 * H),
                      pl.BlockSpec((4, DIFF_HEAD_DIM), lambda b, h, s: (0, 0)),
                      pl.BlockSpec((1, LANES), lambda b, h, s: (0, 0))],
            out_specs=pl.BlockSpec((seq, LANES), lambda b, h, s: (b, h)),
            scratch_shapes=[pltpu.VMEM((TQ, 2 * LANES), F32), pltpu.VMEM((TQ, 2 * LANES), F32),
                            pltpu.VMEM((TQ, 1), F32), pltpu.VMEM((TQ, 1), F32)]),
        out_shape=jax.ShapeDtypeStruct((batch * seq, D_MODEL), BF16),
        compiler_params=_cparams(2), name="diff_attn",
    )(slopes, qkv, qkv, qkv, lam_params, sub_gain)


def _sb_attn_kernel(q_ref, k_ref, v_ref, o_ref, acc, run0, run1, *, seq):
    d = SB_HEAD_DIM
    lane = lax.broadcasted_iota(jnp.int32, (TQ, LANES), 1)
    dn = (((1,), (1,)), ((), ()))

    tk = TK_MID
    r_i = lax.broadcasted_iota(jnp.int32, (tk, tk), 0)
    c_i = lax.broadcasted_iota(jnp.int32, (tk, tk), 1)
    later = jnp.where(r_i > c_i, 1.0, 0.0).astype(BF16)
    vlow = lax.broadcasted_iota(jnp.int32, (tk, LANES), 1) < d
    runs = (run0, run1)

    def q_body(qi, carry):
        qs = pl.multiple_of(qi * TQ, TQ)
        q = q_ref[pl.ds(qs, TQ), :]
        zero = jnp.zeros_like(q)
        qh = (jnp.where(lane < d, q, zero), jnp.where(lane >= d, q, zero))
        qpos = qs + lax.broadcasted_iota(jnp.int32, (TQ, 1), 0)
        acc[...] = jnp.zeros_like(acc)
        run0[...] = jnp.zeros_like(run0)
        run1[...] = jnp.zeros_like(run1)

        def tile(kstart, masked, row0=0):
            k = k_ref[pl.ds(kstart, tk), :]
            v = v_ref[pl.ds(kstart, tk), :]
            vzero = jnp.zeros_like(v)
            vh = (jnp.where(vlow, v, vzero), jnp.where(vlow, vzero, v))
            kpos = kstart + lax.broadcasted_iota(jnp.int32, (1, tk), 1)
            for r0 in range(row0, TQ, ROW_BLOCK):
                r1 = min(r0 + ROW_BLOCK, TQ)
                rows = pl.ds(r0, r1 - r0)
                if masked:
                    before = kpos < qpos[r0:r1]
                total = acc[rows, :]
                for hh in range(2):
                    run = runs[hh]
                    z = lax.dot_general(qh[hh][r0:r1], k, dn, preferred_element_type=F32)
                    drop = jnp.maximum(z, 0.0) + jnp.log2(1.0 + jnp.exp2(-jnp.abs(z)))
                    if masked:
                        drop = jnp.where(before, drop, 0.0)
                    later_drop = jnp.dot(drop.astype(BF16), later, preferred_element_type=F32)
                    r_old = run[rows, :]
                    w = jnp.exp2((z - drop) - (later_drop + r_old))
                    if masked:
                        w = jnp.where(before, w, 0.0)
                    run[rows, :] = r_old + jnp.sum(drop, axis=1, keepdims=True)
                    total = total + jnp.dot(w.astype(BF16), vh[hh], preferred_element_type=F32)
                acc[rows, :] = total

        for c in reversed(range(TQ // tk)):
            tile(pl.multiple_of(qs + c * tk, tk), True, c * tk)
        n_full = qs // tk
        def full_body(j, c):
            tile(pl.multiple_of((n_full - 1 - j) * tk, tk), False)
            return c
        lax.fori_loop(0, n_full, full_body, 0)
        o_ref[pl.ds(qs, TQ), :] = acc[...].astype(o_ref.dtype)
        return carry

    lax.fori_loop(0, seq // TQ, q_body, 0)


def _sb_attention(qkv, batch, seq):
    P = SB_HEADS // 2
    kern = functools.partial(_sb_attn_kernel, seq=seq)
    blk = lambda off: pl.BlockSpec((seq, LANES), lambda b, p: (b, off + p))
    return pl.pallas_call(
        kern, grid=(batch, P),
        in_specs=[blk(0), blk(P), blk(2 * P)],
        out_specs=pl.BlockSpec((seq, LANES), lambda b, p: (b, p)),
        scratch_shapes=[pltpu.VMEM((TQ, LANES), F32), pltpu.VMEM((TQ, 1), F32),
                        pltpu.VMEM((TQ, 1), F32)],
        out_shape=jax.ShapeDtypeStruct((batch * seq, D_MODEL), BF16),
        compiler_params=_cparams(2), name="sb_attn",
    )(qkv, qkv, qkv)


def _proj_res_kernel(o_ref, w_ref, h_ref, out_ref):
    out_ref[...] = h_ref[...] + jnp.dot(o_ref[...], w_ref[...], preferred_element_type=F32)


def _proj_residual(o, w_bf16, h):
    n, d = h.shape
    row = lambda: pl.BlockSpec((ROW_TILE, d), lambda i: (i, 0))
    return pl.pallas_call(
        _proj_res_kernel, grid=(n // ROW_TILE,),
        in_specs=[row(), pl.BlockSpec((d, d), lambda i: (0, 0)), row()],
        out_specs=row(), out_shape=jax.ShapeDtypeStruct((n, d), F32),
        compiler_params=_cparams(1), name="proj_residual",
    )(o, w_bf16, h)


def _split3(x):
    hi = x.astype(BF16)
    return hi, (x - hi.astype(F32)).astype(BF16)


def _router_kernel(h_ref, g_ref, whi_ref, wlo_ref, b_ref, o_ref):
    xn = _rms(h_ref[...], g_ref[...])
    xhi, xlo = _split3(xn)
    logits = (jnp.dot(xhi, whi_ref[...], preferred_element_type=F32)
              + jnp.dot(xhi, wlo_ref[...], preferred_element_type=F32)
              + jnp.dot(xlo, whi_ref[...], preferred_element_type=F32)) + b_ref[...]
    rows = logits.shape[0]
    lane_i = lax.broadcasted_iota(jnp.int32, (rows, LANES), 1)
    lane = lane_i.astype(F32)
    ninf = -jnp.inf
    gl = jnp.where(lane_i < N_GROUPS, logits, ninf)
    gmax = jnp.max(gl, axis=1, keepdims=True)
    gidx = jnp.min(jnp.where(gl == gmax, lane, float(LANES)), axis=1, keepdims=True)
    gsum = jnp.sum(jnp.exp(gl - gmax), axis=1, keepdims=True)
    g_p = 1.0 / gsum
    e_lane = lane_i - N_GROUPS
    lane_group = jnp.right_shift(e_lane, EXPERT_SHIFT).astype(F32)
    in_group = (e_lane >= 0) & (e_lane < N_EXPERTS) & (lane_group == gidx)
    el = jnp.where(in_group, logits, ninf)
    v1 = jnp.max(el, axis=1, keepdims=True)
    i1 = jnp.min(jnp.where(el == v1, lane, float(LANES)), axis=1, keepdims=True)
    el2 = jnp.where(lane == i1, ninf, el)
    v2 = jnp.max(el2, axis=1, keepdims=True)
    i2 = jnp.min(jnp.where(el2 == v2, lane, float(LANES)), axis=1, keepdims=True)
    t = jnp.exp(v2 - v1)
    w1 = g_p / (1.0 + t)
    w2 = g_p * t / (1.0 + t)
    out = jnp.where(lane_i == 0, i1 - N_GROUPS,
          jnp.where(lane_i == 1, i2 - N_GROUPS,
          jnp.where(lane_i == 2, w1, jnp.where(lane_i == 3, w2, 0.0))))
    o_ref[...] = out


def _router(h, gain, w_hi, w_lo, bias):
    n, d = h.shape
    return pl.pallas_call(
        _router_kernel, grid=(n // ROW_TILE,),
        in_specs=[pl.BlockSpec((ROW_TILE, d), lambda i: (i, 0)),
                  pl.BlockSpec((1, d), lambda i: (0, 0)),
                  pl.BlockSpec((d, LANES), lambda i: (0, 0)),
                  pl.BlockSpec((d, LANES), lambda i: (0, 0)),
                  pl.BlockSpec((1, LANES), lambda i: (0, 0))],
        out_specs=pl.BlockSpec((ROW_TILE, LANES), lambda i: (i, 0)),
        out_shape=jax.ShapeDtypeStruct((n, LANES), F32),
        compiler_params=_cparams(1), name="moe_router",
    )(h, gain, w_hi, w_lo, bias)


def _expert_kernel(te_ref, tok_hbm, dst_hbm, h_hbm, g_ref, wg_ref, wu_ref, wd_ref, y_hbm,
                   xbuf0, xbuf1, ybuf0, ybuf1, tok0, tok1, dst0, dst1, isem, gsem, ssem,
                   *, n_tiles):
    t = pl.program_id(0)
    R = EXPERT_TILE
    G = R // SUBLANES
    xbufs, ybufs, toks, dsts = (xbuf0, xbuf1), (ybuf0, ybuf1), (tok0, tok1), (dst0, dst1)

    def idx_copies(tile, s):
        return (pltpu.make_async_copy(tok_hbm.at[tile], toks[s], isem.at[s]),
                pltpu.make_async_copy(dst_hbm.at[tile], dsts[s], isem.at[s]))

    def start_gather(s):
        def body(g, c):
            base = pl.multiple_of(g * SUBLANES, SUBLANES)
            for u in range(SUBLANES):
                row = toks[s][base + u]
                src = h_hbm.at[lax.shift_right_logical(row, SUBLANE_SHIFT),
                               pl.ds(row & (SUBLANES - 1), 1)]
                pltpu.make_async_copy(src, xbufs[s].at[g, pl.ds(u, 1)], gsem.at[s]).start()
            return c
        lax.fori_loop(0, G, body, 0)

    def start_scatter(s):
        def body(g, c):
            base = pl.multiple_of(g * SUBLANES, SUBLANES)
            for u in range(SUBLANES):
                row = dsts[s][base + u]
                dst = y_hbm.at[lax.shift_right_logical(row, SUBLANE_SHIFT),
                               pl.ds(row & (SUBLANES - 1), 1)]
                pltpu.make_async_copy(ybufs[s].at[g, pl.ds(u, 1)], dst, ssem.at[s]).start()
            return c
        lax.fori_loop(0, G, body, 0)

    def wait_gather(s):
        pltpu.make_async_copy(h_hbm.at[pl.ds(0, G)], xbufs[s], gsem.at[s]).wait()

    def wait_scatter(s):
        pltpu.make_async_copy(ybufs[s], y_hbm.at[pl.ds(0, G)], ssem.at[s]).wait()

    def step(s):
        @pl.when(t == 0)
        def _():
            for cp in idx_copies(0, 0):
                cp.start()
            for cp in idx_copies(0, 0):
                cp.wait()
            start_gather(0)
            if n_tiles > 1:
                for cp in idx_copies(1, 1):
                    cp.start()

        @pl.when(t + 1 < n_tiles)
        def _():
            for cp in idx_copies(t + 1, 1 - s):
                cp.wait()
            start_gather(1 - s)

        wait_gather(s)

        @pl.when(t >= 2)
        def _():
            wait_scatter(s)

        x = xbufs[s][...].reshape(R, -1)
        xn = _rms(x, g_ref[...]).astype(BF16)
        a = jnp.dot(xn, wg_ref[0], preferred_element_type=F32)
        u = jnp.dot(xn, wu_ref[0], preferred_element_type=F32)
        hid = (a / (1.0 + jnp.exp(-a))) * u
        y = jnp.dot(hid.astype(BF16), wd_ref[0], preferred_element_type=F32)
        ybufs[s][...] = y.reshape(G, SUBLANES, -1)

        start_scatter(s)

        @pl.when(t + 2 < n_tiles)
        def _():
            for cp in idx_copies(t + 2, s):
                cp.start()

        @pl.when(t == n_tiles - 1)
        def _():
            if n_tiles > 1:
                wait_scatter(1 - s)
            wait_scatter(s)

    for s in range(2):
        pl.when(lax.rem(t, 2) == s)(functools.partial(step, s))


def _expert_mlp(tile_expert, slot_token, slot_dest, h, gain, wg, wu, wd, n_out_rows):
    n_tiles = slot_token.shape[0]
    n, d = h.shape
    f = wg.shape[2]
    R = EXPERT_TILE
    G = R // SUBLANES
    kern = functools.partial(_expert_kernel, n_tiles=n_tiles)
    anyspec = pl.BlockSpec(memory_space=pl.ANY)
    buf = pltpu.VMEM((G, SUBLANES, d), F32)
    idx = pltpu.SMEM((R,), jnp.int32)
    sems = pltpu.SemaphoreType.DMA((2,))
    y = pl.pallas_call(
        kern,
        grid_spec=pltpu.PrefetchScalarGridSpec(
            num_scalar_prefetch=1, grid=(n_tiles,),
            in_specs=[anyspec, anyspec, anyspec,
                      pl.BlockSpec((1, d), lambda t, te: (0, 0)),
                      pl.BlockSpec((1, d, f), lambda t, te: (te[t], 0, 0)),
                      pl.BlockSpec((1, d, f), lambda t, te: (te[t], 0, 0)),
                      pl.BlockSpec((1, f, d), lambda t, te: (te[t], 0, 0))],
            out_specs=anyspec,
            scratch_shapes=[buf, buf, buf, buf, idx, idx, idx, idx, sems, sems, sems]),
        out_shape=jax.ShapeDtypeStruct((n_out_rows // SUBLANES, SUBLANES, d), F32),
        compiler_params=_cparams(1), name="moe_expert",
    )(tile_expert, slot_token, slot_dest, h.reshape(n // SUBLANES, SUBLANES, d), gain, wg, wu, wd)
    return y.reshape(n_out_rows, d)


def _combine_kernel(ya_ref, yb_ref, h_ref, r_ref, o_ref):
    route = r_ref[...]
    o_ref[...] = h_ref[...] + route[:, 2:3] * ya_ref[...] + route[:, 3:4] * yb_ref[...]


def _combine(y, h, route):
    n, d = h.shape
    steps = n // ROW_TILE
    row = lambda off: pl.BlockSpec((ROW_TILE, d), lambda i: (off + i, 0))
    return pl.pallas_call(
        _combine_kernel, grid=(steps,),
        in_specs=[row(0), row(steps), row(0), pl.BlockSpec((ROW_TILE, LANES), lambda i: (i, 0))],
        out_specs=row(0), out_shape=jax.ShapeDtypeStruct((n, d), F32),
        compiler_params=_cparams(1), name="moe_combine",
    )(y, y, h, route)


def _moe(h, gain, w_group, b_group, w_router, b_router, w_gate, w_up, w_down):
    n, d = h.shape
    pad = LANES - N_GROUPS - N_EXPERTS
    w_cat = jnp.concatenate([w_group, w_router, jnp.zeros((d, pad), F32)], axis=1)
    b_cat = jnp.concatenate([b_group, b_router, jnp.zeros((pad,), F32)])[None, :]
    w_hi = w_cat.astype(BF16)
    w_lo = (w_cat - w_hi.astype(F32)).astype(BF16)
    route = _router(h, gain, w_hi, w_lo, b_cat)

    e_flat = route[:, 0:2].astype(jnp.int32).reshape(-1)
    onehot = (e_flat[:, None] == jnp.arange(N_EXPERTS, dtype=jnp.int32)[None, :]).astype(jnp.int32)
    csum = jnp.cumsum(onehot, axis=0)
    counts = csum[-1]
    rank = jnp.sum((csum - onehot) * onehot, axis=1)
    ptiles = (counts + EXPERT_TILE - 1) // EXPERT_TILE
    tile_end = jnp.cumsum(ptiles)
    pstart = (tile_end - ptiles) * EXPERT_TILE
    pos = pstart[e_flat] + rank
    n_tiles = (2 * n) // EXPERT_TILE + N_EXPERTS
    n_slots = n_tiles * EXPERT_TILE
    slot_pair = jnp.full((n_slots,), -1, jnp.int32).at[pos].set(
        jnp.arange(2 * n, dtype=jnp.int32))
    is_pad = slot_pair < 0
    slot_token = jnp.where(is_pad, 0, slot_pair // 2)
    slot_dest = jnp.where(is_pad, 2 * n - 1 + jnp.cumsum(is_pad.astype(jnp.int32)),
                          (slot_pair % 2) * n + slot_pair // 2)
    tile_ids = jnp.arange(n_tiles, dtype=jnp.int32)
    tile_expert = jnp.minimum(
        jnp.sum((tile_ids[:, None] >= tile_end[None, :]).astype(jnp.int32), axis=1),
        N_EXPERTS - 1).astype(jnp.int32)
    n_valid = tile_end[-1]
    last_expert = tile_expert[jnp.maximum(n_valid - 1, 0)]
    tile_expert = jnp.where(tile_ids < n_valid, tile_expert, last_expert)

    y = _expert_mlp(tile_expert, slot_token.reshape(n_tiles, EXPERT_TILE),
                    slot_dest.reshape(n_tiles, EXPERT_TILE), h, gain,
                    w_gate.astype(BF16), w_up.astype(BF16), w_down.astype(BF16), n_slots)
    return _combine(y, h, route)


def kernel(x, meta_tokens, norm_mix, norm_ffn, diff_w_qkv, diff_q_gain, diff_k_gain,
           diff_lambda_q1, diff_lambda_k1, diff_lambda_q2, diff_lambda_k2, diff_sub_gain,
           diff_w_o, sb_w_qkv, sb_w_o, moe_w_group, moe_b_group, moe_w_router,
           moe_b_router, moe_w_gate, moe_w_up, moe_w_down):
    B, S, D = x.shape
    L = N_META + S
    Lp = -(-L // TQ) * TQ
    depth = norm_mix.shape[0]
    assert D == D_MODEL and TQ % Q_BLOCK == 0 and TQ % TK_MID == 0 and TK_BIG == 2 * TK_MID
    assert (B * Lp) % ROW_TILE == 0
    assert (2 * B * Lp) % EXPERT_TILE == 0

    meta = jnp.broadcast_to(meta_tokens.astype(x.dtype)[None], (B, N_META, D))
    h = jnp.concatenate([meta, x, jnp.zeros((B, Lp - L, D), x.dtype)], axis=1).reshape(B * Lp, D)

    blockdiag = (jnp.arange(MXU_DIM)[:, None] // DIFF_HEAD_DIM
                 == jnp.arange(MXU_DIM)[None, :] // DIFF_HEAD_DIM).astype(BF16)
    slopes = jnp.exp2(-8.0 * (jnp.arange(DIFF_HEADS, dtype=F32) + 1.0) / DIFF_HEADS)

    for i in range(depth):
        j = i // N_MIXERS
        gain = norm_mix[i][None, :]
        if i % N_MIXERS == 0:
            lambda_init = 0.8 - 0.6 * math.exp(-0.3 * i)
            scale = DIFF_HEAD_DIM ** -0.5 * LOG2E
            reps = D // DIFF_HEAD_DIM
            qg = (jnp.tile(diff_q_gain[j], reps) * scale)[None, :]
            kg = jnp.tile(diff_k_gain[j], reps)[None, :]
            qkv = _norm_qkv(h, gain, diff_w_qkv[j].astype(BF16), (qg, kg, blockdiag))
            lam_params = jnp.stack([diff_lambda_q1[j], diff_lambda_k1[j],
                                    diff_lambda_q2[j], diff_lambda_k2[j]])
            o = _diff_attention(qkv, slopes, lam_params, diff_sub_gain[j][None, :], B, Lp,
                                lambda_init)
            h = _proj_residual(o, diff_w_o[j].astype(BF16), h)
        else:
            scale = SB_HEAD_DIM ** -0.5 * LOG2E
            colscale = jnp.concatenate([jnp.full((D,), scale, F32), jnp.ones((2 * D,), F32)])
            w = (sb_w_qkv[j] * colscale[None, :]).astype(BF16)
            qkv = _norm_qkv(h, gain, w)
            o = _sb_attention(qkv, B, Lp)
            h = _proj_residual(o, sb_w_o[j].astype(BF16), h)
        h = _moe(h, norm_ffn[i][None, :], moe_w_group[i], moe_b_group[i], moe_w_router[i],
                 moe_b_router[i], moe_w_gate[i], moe_w_up[i], moe_w_down[i])
    return h.reshape(B, Lp, D)[:, N_META:N_META + S]
```

```python
import functools
import math

import jax
import jax.numpy as jnp
from jax import lax
from jax.experimental import pallas as pl
from jax.experimental.pallas import tpu as pltpu

F32 = jnp.float32
BF16 = jnp.bfloat16

D_MODEL = 1024
CHUNK = 64
CHUNK_SHIFT = 6
N_META = 16
Q_BLOCK = 128
N_MIXERS = 2
DIFF_HEADS = 8
DIFF_HEAD_DIM = D_MODEL // (2 * DIFF_HEADS)
SB_HEADS = 16
SB_HEAD_DIM = D_MODEL // SB_HEADS
N_GROUPS = 4
EXPERTS_PER_GROUP = 8
EXPERT_SHIFT = 3
N_EXPERTS = N_GROUPS * EXPERTS_PER_GROUP
D_EXPERT = 512
NORM_EPS = 1e-6

LANES = 128
SUBLANES = 8
SUBLANE_SHIFT = 3
MXU_DIM = 256
VMEM_LIMIT = 56 * 1024 * 1024

ROW_TILE = 512
TQ = 768
TK_BIG = TQ
TK_MID = MXU_DIM
TK_SMALL = 128
ROW_BLOCK = 384
LOG2E = math.log2(math.e)
EXPERT_TILE = 256
DMA_UNROLL = 8
NEG_BIG = -1e30


def _cparams(n_axes):
    return pltpu.CompilerParams(
        dimension_semantics=("arbitrary",) * n_axes,
        vmem_limit_bytes=VMEM_LIMIT,
    )


def _rms(x, gain):
    ms = jnp.mean(x * x, axis=-1, keepdims=True)
    return x * lax.rsqrt(ms + NORM_EPS) * gain


def _norm_qkv_diff_kernel(h_ref, g_ref, w_ref, qg_ref, kg_ref, gm_ref, o_ref):
    D = D_MODEL
    xn = _rms(h_ref[...], g_ref[...]).astype(BF16)
    for part, gain_ref in ((0, qg_ref), (1, kg_ref)):
        y = jnp.dot(xn, w_ref[:, part * D:(part + 1) * D], preferred_element_type=F32)
        y2 = (y * y).astype(BF16)
        ss = jnp.concatenate(
            [jnp.dot(y2[:, c * MXU_DIM:(c + 1) * MXU_DIM], gm_ref[...],
                     preferred_element_type=F32) for c in range(D // MXU_DIM)], axis=1)
        yn = y * lax.rsqrt(ss * (1.0 / DIFF_HEAD_DIM) + NORM_EPS) * gain_ref[...]
        o_ref[:, part * D:(part + 1) * D] = yn.astype(BF16)
    v = jnp.dot(xn, w_ref[:, 2 * D:3 * D], preferred_element_type=F32)
    o_ref[:, 2 * D:3 * D] = v.astype(BF16)


def _norm_qkv_sb_kernel(h_ref, g_ref, w_ref, o_ref):
    D = D_MODEL
    xn = _rms(h_ref[...], g_ref[...]).astype(BF16)
    for part in range(3):
        y = jnp.dot(xn, w_ref[:, part * D:(part + 1) * D], preferred_element_type=F32)
        o_ref[:, part * D:(part + 1) * D] = y.astype(BF16)


def _norm_qkv(h, gain, w_bf16, qk=None):
    n, d = h.shape
    grid = (n // ROW_TILE,)
    row = pl.BlockSpec((ROW_TILE, d), lambda i: (i, 0))
    vec = pl.BlockSpec((1, d), lambda i: (0, 0))
    wspec = pl.BlockSpec((d, 3 * d), lambda i: (0, 0))
    out = pl.BlockSpec((ROW_TILE, 3 * d), lambda i: (i, 0))
    out_shape = jax.ShapeDtypeStruct((n, 3 * d), BF16)
    if qk is None:
        return pl.pallas_call(
            _norm_qkv_sb_kernel, grid=grid, in_specs=[row, vec, wspec], out_specs=out,
            out_shape=out_shape, compiler_params=_cparams(1), name="norm_qkv_sb",
        )(h, gain, w_bf16)
    qg, kg, gm = qk
    gspec = pl.BlockSpec((MXU_DIM, MXU_DIM), lambda i: (0, 0))
    return pl.pallas_call(
        _norm_qkv_diff_kernel, grid=grid, in_specs=[row, vec, wspec, vec, vec, gspec],
        out_specs=out, out_shape=out_shape, compiler_params=_cparams(1), name="norm_qkv_diff",
    )(h, gain, w_bf16, qg, kg, gm)


def _diff_attn_kernel(slopes_ref, q_ref, k_ref, v_ref, lam_ref, sg_ref, o_ref,
                      acc0, acc1, m0, m1, *, seq, lambda_init):
    hd = pl.program_id(1)
    slope = slopes_ref[hd] * LOG2E
    accs, ms = (acc0, acc1), (m0, m1)
    lp = lam_ref[...]
    lam = (jnp.exp(jnp.sum(lp[0:1] * lp[1:2], axis=-1, keepdims=True))
           - jnp.exp(jnp.sum(lp[2:3] * lp[3:4], axis=-1, keepdims=True)) + lambda_init)
    ones = jnp.ones((TK_BIG, LANES), BF16)
    lane = lax.broadcasted_iota(jnp.int32, (TQ, LANES), 1)
    dn = (((1,), (1,)), ((), ()))

    def q_body(qi, carry):
        qs = pl.multiple_of(qi * TQ, TQ)
        q = q_ref[pl.ds(qs, TQ), :]
        zero = jnp.zeros_like(q)
        qc = (jnp.where(lane < DIFF_HEAD_DIM, q, zero), jnp.where(lane >= DIFF_HEAD_DIM, q, zero))
        qpos = qs + lax.broadcasted_iota(jnp.int32, (TQ, 1), 0)
        bound = ((qpos + (CHUNK - N_META)) >> CHUNK_SHIFT) * CHUNK + N_META
        for acc, m in ((acc0, m0), (acc1, m1)):
            acc[...] = jnp.zeros_like(acc)
            m[...] = jnp.full_like(m, NEG_BIG)

        def tile(kstart, tk, masked, row0=0):
            k = k_ref[pl.ds(kstart, tk), :]
            v = v_ref[pl.ds(kstart, tk), :]
            vext = jnp.concatenate([v, ones[:tk]], axis=1)
            kpos = kstart + lax.broadcasted_iota(jnp.int32, (1, tk), 1)
            rel = (kpos - qs).astype(F32) * slope
            for r0 in range(row0, TQ, ROW_BLOCK):
                r1 = min(r0 + ROW_BLOCK, TQ)
                rows = pl.ds(r0, r1 - r0)
                if masked:
                    ahead = jnp.maximum(kpos - qpos[r0:r1], 0).astype(F32) * (2.0 * slope)
                    allowed = kpos < bound[r0:r1]
                for c in range(2):
                    acc, m = accs[c], ms[c]
                    s = lax.dot_general(qc[c][r0:r1], k, dn, preferred_element_type=F32) + rel
                    if masked:
                        s = jnp.where(allowed, s - ahead, -jnp.inf)
                    m_old = m[rows, :]
                    m_new = jnp.maximum(m_old, jnp.max(s, axis=1, keepdims=True))
                    alpha = jnp.exp2(m_old - m_new)
                    p = jnp.exp2(s - m_new).astype(BF16)
                    acc[rows, :] = (acc[rows, :] * alpha
                                    + jnp.dot(p, vext, preferred_element_type=F32))
                    m[rows, :] = m_new

        n_big = qi
        def big_body(j, c):
            tile(pl.multiple_of(2 * j * TK_BIG, TK_BIG), TK_BIG, False)
            tile(pl.multiple_of((2 * j + 1) * TK_BIG, TK_BIG), TK_BIG, False)
            return c
        lax.fori_loop(0, n_big // 2, big_body, 0)
        @pl.when(n_big % 2 == 1)
        def _():
            tile(pl.multiple_of((n_big - 1) * TK_BIG, TK_BIG), TK_BIG, False)
        tile(qs, TQ, True)
        @pl.when(qs + TQ < seq)
        def _():
            tile(pl.multiple_of(qs + TQ, TK_SMALL), TK_SMALL, True, TQ - CHUNK)

        a0 = acc0[...]
        a1 = acc1[...]
        o = a0[:, :LANES] / a0[:, LANES:] - lam * (a1[:, :LANES] / a1[:, LANES:])
        o = _rms(o, sg_ref[...]) * (1.0 - lambda_init)
        o_ref[pl.ds(qs, TQ), :] = o.astype(o_ref.dtype)
        return carry

    lax.fori_loop(0, seq // TQ, q_body, 0)


def _diff_attention(qkv, slopes, lam_params, sub_gain, batch, seq, lambda_init):
    H = DIFF_HEADS
    kern = functools.partial(_diff_attn_kernel, seq=seq, lambda_init=lambda_init)
    blk = lambda off: pl.BlockSpec((seq, LANES), lambda b, h, s: (b, off + h))
    return pl.pallas_call(
        kern,
        grid_spec=pltpu.PrefetchScalarGridSpec(
            num_scalar_prefetch=1, grid=(batch, H),
            in_specs=[blk(0), blk(H), blk(2 * H),
                      pl.BlockSpec((4, DIFF_HEAD_DIM), lambda b, h, s: (0, 0)),
                      pl.BlockSpec((1, LANES), lambda b, h, s: (0, 0))],
            out_specs=pl.BlockSpec((seq, LANES), lambda b, h, s: (b, h)),
            scratch_shapes=[pltpu.VMEM((TQ, 2 * LANES), F32), pltpu.VMEM((TQ, 2 * LANES), F32),
                            pltpu.VMEM((TQ, 1), F32), pltpu.VMEM((TQ, 1), F32)]),
        out_shape=jax.ShapeDtypeStruct((batch * seq, D_MODEL), BF16),
        compiler_params=_cparams(2), name="diff_attn",
    )(slopes, qkv, qkv, qkv, lam_params, sub_gain)


def _sb_attn_kernel(q_ref, k_ref, v_ref, o_ref, acc, run0, run1, *, seq):
    d = SB_HEAD_DIM
    lane = lax.broadcasted_iota(jnp.int32, (TQ, LANES), 1)
    dn = (((1,), (1,)), ((), ()))

    tk = TK_MID
    r_i = lax.broadcasted_iota(jnp.int32, (tk, tk), 0)
    c_i = lax.broadcasted_iota(jnp.int32, (tk, tk), 1)
    from_here = jnp.where(r_i >= c_i, 1.0, 0.0).astype(BF16)
    vlow = lax.broadcasted_iota(jnp.int32, (tk, LANES), 1) < d
    runs = (run0, run1)

    def q_body(qi, carry):
        qs = pl.multiple_of(qi * TQ, TQ)
        q = q_ref[pl.ds(qs, TQ), :]
        zero = jnp.zeros_like(q)
        qh = (jnp.where(lane < d, q, zero), jnp.where(lane >= d, q, zero))
        qpos = qs + lax.broadcasted_iota(jnp.int32, (TQ, 1), 0)
        acc[...] = jnp.zeros_like(acc)
        run0[...] = jnp.zeros_like(run0)
        run1[...] = jnp.zeros_like(run1)

        def tile(kstart, masked, row0=0):
            k = k_ref[pl.ds(kstart, tk), :]
            v = v_ref[pl.ds(kstart, tk), :]
            vzero = jnp.zeros_like(v)
            vh = (jnp.where(vlow, v, vzero), jnp.where(vlow, vzero, v))
            kpos = kstart + lax.broadcasted_iota(jnp.int32, (1, tk), 1)
            for r0 in range(row0, TQ, ROW_BLOCK):
                r1 = min(r0 + ROW_BLOCK, TQ)
                rows = pl.ds(r0, r1 - r0)
                if masked:
                    before = kpos < qpos[r0:r1]
                total = acc[rows, :]
                for hh in range(2):
                    run = runs[hh]
                    z = lax.dot_general(qh[hh][r0:r1], k, dn, preferred_element_type=F32)
                    drop = jnp.maximum(z, 0.0) + jnp.log2(1.0 + jnp.exp2(-jnp.abs(z)))
                    if masked:
                        drop = jnp.where(before, drop, 0.0)
                    tail = jnp.dot(drop.astype(BF16), from_here, preferred_element_type=F32)
                    r_old = run[rows, :]
                    w = jnp.exp2((z - tail) - r_old)
                    if masked:
                        w = jnp.where(before, w, 0.0)
                    run[rows, :] = r_old + tail[:, 0:1]
                    total = total + jnp.dot(w.astype(BF16), vh[hh], preferred_element_type=F32)
                acc[rows, :] = total

        for c in reversed(range(TQ // tk)):
            tile(pl.multiple_of(qs + c * tk, tk), True, c * tk)
        n_full = qs // tk
        per = TQ // tk
        def full_body(j, c):
            for i in range(per):
                tile(pl.multiple_of((n_full - 1 - i - per * j) * tk, tk), False)
            return c
        lax.fori_loop(0, n_full // per, full_body, 0)
        o_ref[pl.ds(qs, TQ), :] = acc[...].astype(o_ref.dtype)
        return carry

    lax.fori_loop(0, seq // TQ, q_body, 0)


def _sb_attention(qkv, batch, seq):
    P = SB_HEADS // 2
    kern = functools.partial(_sb_attn_kernel, seq=seq)
    blk = lambda off: pl.BlockSpec((seq, LANES), lambda b, p: (b, off + p))
    return pl.pallas_call(
        kern, grid=(batch, P),
        in_specs=[blk(0), blk(P), blk(2 * P)],
        out_specs=pl.BlockSpec((seq, LANES), lambda b, p: (b, p)),
        scratch_shapes=[pltpu.VMEM((TQ, LANES), F32), pltpu.VMEM((TQ, 1), F32),
                        pltpu.VMEM((TQ, 1), F32)],
        out_shape=jax.ShapeDtypeStruct((batch * seq, D_MODEL), BF16),
        compiler_params=_cparams(2), name="sb_attn",
    )(qkv, qkv, qkv)


def _proj_res_kernel(o_ref, w_ref, h_ref, out_ref):
    out_ref[...] = h_ref[...] + jnp.dot(o_ref[...], w_ref[...], preferred_element_type=F32)


def _proj_residual(o, w_bf16, h):
    n, d = h.shape
    row = lambda: pl.BlockSpec((ROW_TILE, d), lambda i: (i, 0))
    return pl.pallas_call(
        _proj_res_kernel, grid=(n // ROW_TILE,),
        in_specs=[row(), pl.BlockSpec((d, d), lambda i: (0, 0)), row()],
        out_specs=row(), out_shape=jax.ShapeDtypeStruct((n, d), F32),
        compiler_params=_cparams(1), name="proj_residual",
    )(o, w_bf16, h)


def _split3(x):
    hi = x.astype(BF16)
    return hi, (x - hi.astype(F32)).astype(BF16)


def _router_kernel(h_ref, g_ref, whi_ref, wlo_ref, b_ref, o_ref):
    xn = _rms(h_ref[...], g_ref[...])
    xhi, xlo = _split3(xn)
    logits = (jnp.dot(xhi, whi_ref[...], preferred_element_type=F32)
              + jnp.dot(xhi, wlo_ref[...], preferred_element_type=F32)
              + jnp.dot(xlo, whi_ref[...], preferred_element_type=F32)) + b_ref[...]
    rows = logits.shape[0]
    lane_i = lax.broadcasted_iota(jnp.int32, (rows, LANES), 1)
    lane = lane_i.astype(F32)
    ninf = -jnp.inf
    gl = jnp.where(lane_i < N_GROUPS, logits, ninf)
    gmax = jnp.max(gl, axis=1, keepdims=True)
    gidx = jnp.min(jnp.where(gl == gmax, lane, float(LANES)), axis=1, keepdims=True)
    gsum = jnp.sum(jnp.exp(gl - gmax), axis=1, keepdims=True)
    g_p = 1.0 / gsum
    e_lane = lane_i - N_GROUPS
    lane_group = jnp.right_shift(e_lane, EXPERT_SHIFT).astype(F32)
    in_group = (e_lane >= 0) & (e_lane < N_EXPERTS) & (lane_group == gidx)
    el = jnp.where(in_group, logits, ninf)
    v1 = jnp.max(el, axis=1, keepdims=True)
    i1 = jnp.min(jnp.where(el == v1, lane, float(LANES)), axis=1, keepdims=True)
    el2 = jnp.where(lane == i1, ninf, el)
    v2 = jnp.max(el2, axis=1, keepdims=True)
    i2 = jnp.min(jnp.where(el2 == v2, lane, float(LANES)), axis=1, keepdims=True)
    t = jnp.exp(v2 - v1)
    w1 = g_p / (1.0 + t)
    w2 = g_p * t / (1.0 + t)
    out = jnp.where(lane_i == 0, i1 - N_GROUPS,
          jnp.where(lane_i == 1, i2 - N_GROUPS,
          jnp.where(lane_i == 2, w1, jnp.where(lane_i == 3, w2, 0.0))))
    o_ref[...] = out


def _router(h, gain, w_hi, w_lo, bias):
    n, d = h.shape
    return pl.pallas_call(
        _router_kernel, grid=(n // ROW_TILE,),
        in_specs=[pl.BlockSpec((ROW_TILE, d), lambda i: (i, 0)),
                  pl.BlockSpec((1, d), lambda i: (0, 0)),
                  pl.BlockSpec((d, LANES), lambda i: (0, 0)),
                  pl.BlockSpec((d, LANES), lambda i: (0, 0)),
                  pl.BlockSpec((1, LANES), lambda i: (0, 0))],
        out_specs=pl.BlockSpec((ROW_TILE, LANES), lambda i: (i, 0)),
        out_shape=jax.ShapeDtypeStruct((n, LANES), F32),
        compiler_params=_cparams(1), name="moe_router",
    )(h, gain, w_hi, w_lo, bias)


def _expert_kernel(te_ref, tok_hbm, dst_hbm, h_hbm, g_ref, wg_ref, wu_ref, wd_ref, y_hbm,
                   xbuf0, xbuf1, ybuf0, ybuf1, tok0, tok1, dst0, dst1, isem, gsem, ssem,
                   *, n_tiles):
    t = pl.program_id(0)
    R = EXPERT_TILE
    G = R // SUBLANES
    xbufs, ybufs, toks, dsts = (xbuf0, xbuf1), (ybuf0, ybuf1), (tok0, tok1), (dst0, dst1)

    def idx_copies(tile, s):
        return (pltpu.make_async_copy(tok_hbm.at[tile], toks[s], isem.at[s]),
                pltpu.make_async_copy(dst_hbm.at[tile], dsts[s], isem.at[s]))

    def start_gather(s):
        def body(g, c):
            base = pl.multiple_of(g * SUBLANES, SUBLANES)
            for u in range(SUBLANES):
                row = toks[s][base + u]
                src = h_hbm.at[lax.shift_right_logical(row, SUBLANE_SHIFT),
                               pl.ds(row & (SUBLANES - 1), 1)]
                pltpu.make_async_copy(src, xbufs[s].at[g, pl.ds(u, 1)], gsem.at[s]).start()
            return c
        lax.fori_loop(0, G, body, 0)

    def start_scatter(s):
        def body(g, c):
            base = pl.multiple_of(g * SUBLANES, SUBLANES)
            for u in range(SUBLANES):
                row = dsts[s][base + u]
                dst = y_hbm.at[lax.shift_right_logical(row, SUBLANE_SHIFT),
                               pl.ds(row & (SUBLANES - 1), 1)]
                pltpu.make_async_copy(ybufs[s].at[g, pl.ds(u, 1)], dst, ssem.at[s]).start()
            return c
        lax.fori_loop(0, G, body, 0)

    def wait_gather(s):
        pltpu.make_async_copy(h_hbm.at[pl.ds(0, G)], xbufs[s], gsem.at[s]).wait()

    def wait_scatter(s):
        pltpu.make_async_copy(ybufs[s], y_hbm.at[pl.ds(0, G)], ssem.at[s]).wait()

    def step(s):
        @pl.when(t == 0)
        def _():
            for cp in idx_copies(0, 0):
                cp.start()
            for cp in idx_copies(0, 0):
                cp.wait()
            start_gather(0)
            if n_tiles > 1:
                for cp in idx_copies(1, 1):
                    cp.start()

        @pl.when(t + 1 < n_tiles)
        def _():
            for cp in idx_copies(t + 1, 1 - s):
                cp.wait()
            start_gather(1 - s)

        wait_gather(s)

        @pl.when(t >= 2)
        def _():
            wait_scatter(s)

        x = xbufs[s][...].reshape(R, -1)
        xn = _rms(x, g_ref[...]).astype(BF16)
        a = jnp.dot(xn, wg_ref[0], preferred_element_type=F32)
        u = jnp.dot(xn, wu_ref[0], preferred_element_type=F32)
        hid = (a / (1.0 + jnp.exp(-a))) * u
        y = jnp.dot(hid.astype(BF16), wd_ref[0], preferred_element_type=F32)
        ybufs[s][...] = y.reshape(G, SUBLANES, -1)

        start_scatter(s)

        @pl.when(t + 2 < n_tiles)
        def _():
            for cp in idx_copies(t + 2, s):
                cp.start()

        @pl.when(t == n_tiles - 1)
        def _():
            if n_tiles > 1:
                wait_scatter(1 - s)
            wait_scatter(s)

    for s in range(2):
        pl.when(lax.rem(t, 2) == s)(functools.partial(step, s))


def _expert_mlp(tile_expert, slot_token, slot_dest, h, gain, wg, wu, wd, n_out_rows):
    n_tiles = slot_token.shape[0]
    n, d = h.shape
    f = wg.shape[2]
    R = EXPERT_TILE
    G = R // SUBLANES
    kern = functools.partial(_expert_kernel, n_tiles=n_tiles)
    anyspec = pl.BlockSpec(memory_space=pl.ANY)
    buf = pltpu.VMEM((G, SUBLANES, d), F32)
    idx = pltpu.SMEM((R,), jnp.int32)
    sems = pltpu.SemaphoreType.DMA((2,))
    y = pl.pallas_call(
        kern,
        grid_spec=pltpu.PrefetchScalarGridSpec(
            num_scalar_prefetch=1, grid=(n_tiles,),
            in_specs=[anyspec, anyspec, anyspec,
                      pl.BlockSpec((1, d), lambda t, te: (0, 0)),
                      pl.BlockSpec((1, d, f), lambda t, te: (te[t], 0, 0)),
                      pl.BlockSpec((1, d, f), lambda t, te: (te[t], 0, 0)),
                      pl.BlockSpec((1, f, d), lambda t, te: (te[t], 0, 0))],
            out_specs=anyspec,
            scratch_shapes=[buf, buf, buf, buf, idx, idx, idx, idx, sems, sems, sems]),
        out_shape=jax.ShapeDtypeStruct((n_out_rows // SUBLANES, SUBLANES, d), F32),
        compiler_params=_cparams(1), name="moe_expert",
    )(tile_expert, slot_token, slot_dest, h.reshape(n // SUBLANES, SUBLANES, d), gain, wg, wu, wd)
    return y.reshape(n_out_rows, d)


def _combine_kernel(ya_ref, yb_ref, h_ref, r_ref, o_ref):
    route = r_ref[...]
    o_ref[...] = h_ref[...] + route[:, 2:3] * ya_ref[...] + route[:, 3:4] * yb_ref[...]


def _combine(y, h, route):
    n, d = h.shape
    steps = n // ROW_TILE
    row = lambda off: pl.BlockSpec((ROW_TILE, d), lambda i: (off + i, 0))
    return pl.pallas_call(
        _combine_kernel, grid=(steps,),
        in_specs=[row(0), row(steps), row(0), pl.BlockSpec((ROW_TILE, LANES), lambda i: (i, 0))],
        out_specs=row(0), out_shape=jax.ShapeDtypeStruct((n, d), F32),
        compiler_params=_cparams(1), name="moe_combine",
    )(y, y, h, route)


def _moe(h, gain, w_group, b_group, w_router, b_router, w_gate, w_up, w_down):
    n, d = h.shape
    pad = LANES - N_GROUPS - N_EXPERTS
    w_cat = jnp.concatenate([w_group, w_router, jnp.zeros((d, pad), F32)], axis=1)
    b_cat = jnp.concatenate([b_group, b_router, jnp.zeros((pad,), F32)])[None, :]
    w_hi = w_cat.astype(BF16)
    w_lo = (w_cat - w_hi.astype(F32)).astype(BF16)
    route = _router(h, gain, w_hi, w_lo, b_cat)

    e_flat = route[:, 0:2].astype(jnp.int32).reshape(-1)
    onehot = (e_flat[:, None] == jnp.arange(N_EXPERTS, dtype=jnp.int32)[None, :]).astype(jnp.int32)
    csum = jnp.cumsum(onehot, axis=0)
    counts = csum[-1]
    rank = jnp.sum((csum - onehot) * onehot, axis=1)
    ptiles = (counts + EXPERT_TILE - 1) // EXPERT_TILE
    tile_end = jnp.cumsum(ptiles)
    pstart = (tile_end - ptiles) * EXPERT_TILE
    pos = pstart[e_flat] + rank
    n_tiles = (2 * n) // EXPERT_TILE + N_EXPERTS
    n_slots = n_tiles * EXPERT_TILE
    slot_pair = jnp.full((n_slots,), -1, jnp.int32).at[pos].set(
        jnp.arange(2 * n, dtype=jnp.int32))
    is_pad = slot_pair < 0
    slot_token = jnp.where(is_pad, 0, slot_pair // 2)
    slot_dest = jnp.where(is_pad, 2 * n - 1 + jnp.cumsum(is_pad.astype(jnp.int32)),
                          (slot_pair % 2) * n + slot_pair // 2)
    tile_ids = jnp.arange(n_tiles, dtype=jnp.int32)
    tile_expert = jnp.minimum(
        jnp.sum((tile_ids[:, None] >= tile_end[None, :]).astype(jnp.int32), axis=1),
        N_EXPERTS - 1).astype(jnp.int32)
    n_valid = tile_end[-1]
    last_expert = tile_expert[jnp.maximum(n_valid - 1, 0)]
    tile_expert = jnp.where(tile_ids < n_valid, tile_expert, last_expert)

    y = _expert_mlp(tile_expert, slot_token.reshape(n_tiles, EXPERT_TILE),
                    slot_dest.reshape(n_tiles, EXPERT_TILE), h, gain,
                    w_gate.astype(BF16), w_up.astype(BF16), w_down.astype(BF16), n_slots)
    return _combine(y, h, route)


def kernel(x, meta_tokens, norm_mix, norm_ffn, diff_w_qkv, diff_q_gain, diff_k_gain,
           diff_lambda_q1, diff_lambda_k1, diff_lambda_q2, diff_lambda_k2, diff_sub_gain,
           diff_w_o, sb_w_qkv, sb_w_o, moe_w_group, moe_b_group, moe_w_router,
           moe_b_router, moe_w_gate, moe_w_up, moe_w_down):
    B, S, D = x.shape
    L = N_META + S
    Lp = -(-L // TQ) * TQ
    depth = norm_mix.shape[0]
    assert D == D_MODEL and TQ % Q_BLOCK == 0 and TQ % TK_MID == 0 and TK_BIG == TQ
    assert (B * Lp) % ROW_TILE == 0
    assert (2 * B * Lp) % EXPERT_TILE == 0

    meta = jnp.broadcast_to(meta_tokens.astype(x.dtype)[None], (B, N_META, D))
    h = jnp.concatenate([meta, x, jnp.zeros((B, Lp - L, D), x.dtype)], axis=1).reshape(B * Lp, D)

    blockdiag = (jnp.arange(MXU_DIM)[:, None] // DIFF_HEAD_DIM
                 == jnp.arange(MXU_DIM)[None, :] // DIFF_HEAD_DIM).astype(BF16)
    slopes = jnp.exp2(-8.0 * (jnp.arange(DIFF_HEADS, dtype=F32) + 1.0) / DIFF_HEADS)

    for i in range(depth):
        j = i // N_MIXERS
        gain = norm_mix[i][None, :]
        if i % N_MIXERS == 0:
            lambda_init = 0.8 - 0.6 * math.exp(-0.3 * i)
            scale = DIFF_HEAD_DIM ** -0.5 * LOG2E
            reps = D // DIFF_HEAD_DIM
            qg = (jnp.tile(diff_q_gain[j], reps) * scale)[None, :]
            kg = jnp.tile(diff_k_gain[j], reps)[None, :]
            qkv = _norm_qkv(h, gain, diff_w_qkv[j].astype(BF16), (qg, kg, blockdiag))
            lam_params = jnp.stack([diff_lambda_q1[j], diff_lambda_k1[j],
                                    diff_lambda_q2[j], diff_lambda_k2[j]])
            o = _diff_attention(qkv, slopes, lam_params, diff_sub_gain[j][None, :], B, Lp,
                                lambda_init)
            h = _proj_residual(o, diff_w_o[j].astype(BF16), h)
        else:
            scale = SB_HEAD_DIM ** -0.5 * LOG2E
            colscale = jnp.concatenate([jnp.full((D,), scale, F32), jnp.ones((2 * D,), F32)])
            w = (sb_w_qkv[j] * colscale[None, :]).astype(BF16)
            qkv = _norm_qkv(h, gain, w)
            o = _sb_attention(qkv, B, Lp)
            h = _proj_residual(o, sb_w_o[j].astype(BF16), h)
        h = _moe(h, norm_ffn[i][None, :], moe_w_group[i], moe_b_group[i], moe_w_router[i],
                 moe_b_router[i], moe_w_gate[i], moe_w_up[i], moe_w_down[i])
    return h.reshape(B, Lp, D)[:, N_META:N_META + S]
```

```python
import functools
import math

import jax
import jax.numpy as jnp
from jax import lax
from jax.experimental import pallas as pl
from jax.experimental.pallas import tpu as pltpu

F32 = jnp.float32
BF16 = jnp.bfloat16

D_MODEL = 1024
CHUNK = 64
CHUNK_SHIFT = 6
N_META = 16
Q_BLOCK = 128
N_MIXERS = 2
DIFF_HEADS = 8
DIFF_HEAD_DIM = D_MODEL // (2 * DIFF_HEADS)
SB_HEADS = 16
SB_HEAD_DIM = D_MODEL // SB_HEADS
N_GROUPS = 4
EXPERTS_PER_GROUP = 8
EXPERT_SHIFT = 3
N_EXPERTS = N_GROUPS * EXPERTS_PER_GROUP
D_EXPERT = 512
NORM_EPS = 1e-6

LANES = 128
SUBLANES = 8
SUBLANE_SHIFT = 3
MXU_DIM = 256
VMEM_LIMIT = 56 * 1024 * 1024

ROW_TILE = 512
TQ = 768
TK_BIG = TQ
TK_MID = MXU_DIM
TK_SMALL = 128
ROW_BLOCK = 384
LOG2E = math.log2(math.e)
SOFTPLUS_CAP = 64.0
EXPERT_TILE = 256
DMA_PRIORITIES = 2
NEG_BIG = -1e30


def _cparams(n_axes):
    return pltpu.CompilerParams(
        dimension_semantics=("arbitrary",) * n_axes,
        vmem_limit_bytes=VMEM_LIMIT,
    )


def _rms(x, gain):
    ms = jnp.mean(x * x, axis=-1, keepdims=True)
    return x * lax.rsqrt(ms + NORM_EPS) * gain


def _norm_qkv_diff_kernel(h_ref, g_ref, w_ref, qg_ref, kg_ref, gm_ref, o_ref):
    D = D_MODEL
    xn = _rms(h_ref[...], g_ref[...]).astype(BF16)
    for part, gain_ref in ((0, qg_ref), (1, kg_ref)):
        y = jnp.dot(xn, w_ref[:, part * D:(part + 1) * D], preferred_element_type=F32)
        y2 = (y * y).astype(BF16)
        ss = jnp.concatenate(
            [jnp.dot(y2[:, c * MXU_DIM:(c + 1) * MXU_DIM], gm_ref[...],
                     preferred_element_type=F32) for c in range(D // MXU_DIM)], axis=1)
        yn = y * lax.rsqrt(ss * (1.0 / DIFF_HEAD_DIM) + NORM_EPS) * gain_ref[...]
        o_ref[:, part * D:(part + 1) * D] = yn.astype(BF16)
    v = jnp.dot(xn, w_ref[:, 2 * D:3 * D], preferred_element_type=F32)
    o_ref[:, 2 * D:3 * D] = v.astype(BF16)


def _norm_qkv_sb_kernel(h_ref, g_ref, w_ref, o_ref):
    D = D_MODEL
    xn = _rms(h_ref[...], g_ref[...]).astype(BF16)
    for part in range(3):
        y = jnp.dot(xn, w_ref[:, part * D:(part + 1) * D], preferred_element_type=F32)
        o_ref[:, part * D:(part + 1) * D] = y.astype(BF16)


def _norm_qkv(h, gain, w_bf16, qk=None):
    n, d = h.shape
    grid = (n // ROW_TILE,)
    row = pl.BlockSpec((ROW_TILE, d), lambda i: (i, 0))
    vec = pl.BlockSpec((1, d), lambda i: (0, 0))
    wspec = pl.BlockSpec((d, 3 * d), lambda i: (0, 0))
    out = pl.BlockSpec((ROW_TILE, 3 * d), lambda i: (i, 0))
    out_shape = jax.ShapeDtypeStruct((n, 3 * d), BF16)
    if qk is None:
        return pl.pallas_call(
            _norm_qkv_sb_kernel, grid=grid, in_specs=[row, vec, wspec], out_specs=out,
            out_shape=out_shape, compiler_params=_cparams(1), name="norm_qkv_sb",
        )(h, gain, w_bf16)
    qg, kg, gm = qk
    gspec = pl.BlockSpec((MXU_DIM, MXU_DIM), lambda i: (0, 0))
    return pl.pallas_call(
        _norm_qkv_diff_kernel, grid=grid, in_specs=[row, vec, wspec, vec, vec, gspec],
        out_specs=out, out_shape=out_shape, compiler_params=_cparams(1), name="norm_qkv_diff",
    )(h, gain, w_bf16, qg, kg, gm)


def _diff_attn_kernel(slopes_ref, q_ref, k_ref, v_ref, lam_ref, sg_ref, o_ref,
                      acc0, acc1, m0, m1, *, seq, lambda_init):
    hd = pl.program_id(1)
    slope = slopes_ref[hd] * LOG2E
    accs, ms = (acc0, acc1), (m0, m1)
    lp = lam_ref[...]
    lam = (jnp.exp(jnp.sum(lp[0:1] * lp[1:2], axis=-1, keepdims=True))
           - jnp.exp(jnp.sum(lp[2:3] * lp[3:4], axis=-1, keepdims=True)) + lambda_init)
    ones = jnp.ones((TK_BIG, LANES), BF16)
    lane = lax.broadcasted_iota(jnp.int32, (TQ, LANES), 1)
    dn = (((1,), (1,)), ((), ()))

    def q_body(qi, carry):
        qs = pl.multiple_of(qi * TQ, TQ)
        q = q_ref[pl.ds(qs, TQ), :]
        zero = jnp.zeros_like(q)
        qc = (jnp.where(lane < DIFF_HEAD_DIM, q, zero), jnp.where(lane >= DIFF_HEAD_DIM, q, zero))
        qpos = qs + lax.broadcasted_iota(jnp.int32, (TQ, 1), 0)
        bound = ((qpos + (CHUNK - N_META)) >> CHUNK_SHIFT) * CHUNK + N_META
        for acc, m in ((acc0, m0), (acc1, m1)):
            acc[...] = jnp.zeros_like(acc)
            m[...] = jnp.full_like(m, NEG_BIG)

        def tile(kstart, tk, masked, row0=0):
            k = k_ref[pl.ds(kstart, tk), :]
            v = v_ref[pl.ds(kstart, tk), :]
            vext = jnp.concatenate([v, ones[:tk]], axis=1)
            kpos = kstart + lax.broadcasted_iota(jnp.int32, (1, tk), 1)
            rel = (kpos - qs).astype(F32) * slope
            for r0 in range(row0, TQ, ROW_BLOCK):
                r1 = min(r0 + ROW_BLOCK, TQ)
                rows = pl.ds(r0, r1 - r0)
                if masked:
                    ahead = jnp.maximum(kpos - qpos[r0:r1], 0).astype(F32) * (2.0 * slope)
                    allowed = kpos < bound[r0:r1]
                for c in range(2):
                    acc, m = accs[c], ms[c]
                    s = lax.dot_general(qc[c][r0:r1], k, dn, preferred_element_type=F32) + rel
                    if masked:
                        s = jnp.where(allowed, s - ahead, -jnp.inf)
                    m_old = m[rows, :]
                    m_new = jnp.maximum(m_old, jnp.max(s, axis=1, keepdims=True))
                    alpha = jnp.exp2(m_old - m_new)
                    p = jnp.exp2(s - m_new).astype(BF16)
                    acc[rows, :] = (acc[rows, :] * alpha
                                    + jnp.dot(p, vext, preferred_element_type=F32))
                    m[rows, :] = m_new

        n_big = qi
        def big_body(j, c):
            tile(pl.multiple_of(2 * j * TK_BIG, TK_BIG), TK_BIG, False)
            tile(pl.multiple_of((2 * j + 1) * TK_BIG, TK_BIG), TK_BIG, False)
            return c
        lax.fori_loop(0, n_big // 2, big_body, 0)
        @pl.when(n_big % 2 == 1)
        def _():
            tile(pl.multiple_of((n_big - 1) * TK_BIG, TK_BIG), TK_BIG, False)
        tile(qs, TQ, True)
        @pl.when(qs + TQ < seq)
        def _():
            tile(pl.multiple_of(qs + TQ, TK_SMALL), TK_SMALL, True, TQ - CHUNK)

        a0 = acc0[...]
        a1 = acc1[...]
        o = a0[:, :LANES] / a0[:, LANES:] - lam * (a1[:, :LANES] / a1[:, LANES:])
        o = _rms(o, sg_ref[...]) * (1.0 - lambda_init)
        o_ref[pl.ds(qs, TQ), :] = o.astype(o_ref.dtype)
        return carry

    lax.fori_loop(0, seq // TQ, q_body, 0)


def _diff_attention(qkv, slopes, lam_params, sub_gain, batch, seq, lambda_init):
    H = DIFF_HEADS
    kern = functools.partial(_diff_attn_kernel, seq=seq, lambda_init=lambda_init)
    blk = lambda off: pl.BlockSpec((seq, LANES), lambda b, h, s: (b, off + h))
    return pl.pallas_call(
        kern,
        grid_spec=pltpu.PrefetchScalarGridSpec(
            num_scalar_prefetch=1, grid=(batch, H),
            in_specs=[blk(0), blk(H), blk(2 * H),
                      pl.BlockSpec((4, DIFF_HEAD_DIM), lambda b, h, s: (0, 0)),
                      pl.BlockSpec((1, LANES), lambda b, h, s: (0, 0))],
            out_specs=pl.BlockSpec((seq, LANES), lambda b, h, s: (b, h)),
            scratch_shapes=[pltpu.VMEM((TQ, 2 * LANES), F32), pltpu.VMEM((TQ, 2 * LANES), F32),
                            pltpu.VMEM((TQ, 1), F32), pltpu.VMEM((TQ, 1), F32)]),
        out_shape=jax.ShapeDtypeStruct((batch * seq, D_MODEL), BF16),
        compiler_params=_cparams(2), name="diff_attn",
    )(slopes, qkv, qkv, qkv, lam_params, sub_gain)


def _sb_attn_kernel(q_ref, k_ref, v_ref, o_ref, acc, run0, run1, *, seq):
    d = SB_HEAD_DIM
    lane = lax.broadcasted_iota(jnp.int32, (TQ, LANES), 1)
    dn = (((1,), (1,)), ((), ()))

    tk = TK_MID
    r_i = lax.broadcasted_iota(jnp.int32, (tk, tk), 0)
    c_i = lax.broadcasted_iota(jnp.int32, (tk, tk), 1)
    from_here = jnp.where(r_i >= c_i, 1.0, 0.0).astype(BF16)
    vlow = lax.broadcasted_iota(jnp.int32, (tk, LANES), 1) < d
    runs = (run0, run1)

    def q_body(qi, carry):
        qs = pl.multiple_of(qi * TQ, TQ)
        q = q_ref[pl.ds(qs, TQ), :]
        zero = jnp.zeros_like(q)
        qh = (jnp.where(lane < d, q, zero), jnp.where(lane >= d, q, zero))
        qpos = qs + lax.broadcasted_iota(jnp.int32, (TQ, 1), 0)
        acc[...] = jnp.zeros_like(acc)
        run0[...] = jnp.zeros_like(run0)
        run1[...] = jnp.zeros_like(run1)

        def tile(kstart, masked, row0=0):
            k = k_ref[pl.ds(kstart, tk), :]
            v = v_ref[pl.ds(kstart, tk), :]
            vzero = jnp.zeros_like(v)
            vh = (jnp.where(vlow, v, vzero), jnp.where(vlow, vzero, v))
            kpos = kstart + lax.broadcasted_iota(jnp.int32, (1, tk), 1)
            for r0 in range(row0, TQ, ROW_BLOCK):
                r1 = min(r0 + ROW_BLOCK, TQ)
                rows = pl.ds(r0, r1 - r0)
                if masked:
                    before = kpos < qpos[r0:r1]
                total = acc[rows, :]
                for hh in range(2):
                    run = runs[hh]
                    z = lax.dot_general(qh[hh][r0:r1], k, dn, preferred_element_type=F32)
                    drop = jnp.maximum(z, jnp.log2(1.0 + jnp.exp2(jnp.minimum(z, SOFTPLUS_CAP))))
                    if masked:
                        drop = jnp.where(before, drop, 0.0)
                    tail = jnp.dot(drop.astype(BF16), from_here, preferred_element_type=F32)
                    r_old = run[rows, :]
                    w = jnp.exp2((z - tail) - r_old)
                    if masked:
                        w = jnp.where(before, w, 0.0)
                    run[rows, :] = r_old + tail[:, 0:1]
                    total = total + jnp.dot(w.astype(BF16), vh[hh], preferred_element_type=F32)
                acc[rows, :] = total

        for c in reversed(range(TQ // tk)):
            tile(pl.multiple_of(qs + c * tk, tk), True, c * tk)
        n_full = qs // tk
        per = TQ // tk
        def full_body(j, c):
            for i in range(per):
                tile(pl.multiple_of((n_full - 1 - i - per * j) * tk, tk), False)
            return c
        lax.fori_loop(0, n_full // per, full_body, 0)
        o_ref[pl.ds(qs, TQ), :] = acc[...].astype(o_ref.dtype)
        return carry

    lax.fori_loop(0, seq // TQ, q_body, 0)


def _sb_attention(qkv, batch, seq):
    P = SB_HEADS // 2
    kern = functools.partial(_sb_attn_kernel, seq=seq)
    blk = lambda off: pl.BlockSpec((seq, LANES), lambda b, p: (b, off + p))
    return pl.pallas_call(
        kern, grid=(batch, P),
        in_specs=[blk(0), blk(P), blk(2 * P)],
        out_specs=pl.BlockSpec((seq, LANES), lambda b, p: (b, p)),
        scratch_shapes=[pltpu.VMEM((TQ, LANES), F32), pltpu.VMEM((TQ, 1), F32),
                        pltpu.VMEM((TQ, 1), F32)],
        out_shape=jax.ShapeDtypeStruct((batch * seq, D_MODEL), BF16),
        compiler_params=_cparams(2), name="sb_attn",
    )(qkv, qkv, qkv)


def _proj_res_kernel(o_ref, w_ref, h_ref, out_ref):
    out_ref[...] = h_ref[...] + jnp.dot(o_ref[...], w_ref[...], preferred_element_type=F32)


def _proj_residual(o, w_bf16, h):
    n, d = h.shape
    row = lambda: pl.BlockSpec((ROW_TILE, d), lambda i: (i, 0))
    return pl.pallas_call(
        _proj_res_kernel, grid=(n // ROW_TILE,),
        in_specs=[row(), pl.BlockSpec((d, d), lambda i: (0, 0)), row()],
        out_specs=row(), out_shape=jax.ShapeDtypeStruct((n, d), F32),
        compiler_params=_cparams(1), name="proj_residual",
    )(o, w_bf16, h)


def _split3(x):
    hi = x.astype(BF16)
    return hi, (x - hi.astype(F32)).astype(BF16)


def _router_kernel(h_ref, g_ref, whi_ref, wlo_ref, b_ref, o_ref, cnt_ref):
    xn = _rms(h_ref[...], g_ref[...])
    xhi, xlo = _split3(xn)
    logits = (jnp.dot(xhi, whi_ref[...], preferred_element_type=F32)
              + jnp.dot(xhi, wlo_ref[...], preferred_element_type=F32)
              + jnp.dot(xlo, whi_ref[...], preferred_element_type=F32)) + b_ref[...]
    rows = logits.shape[0]
    lane_i = lax.broadcasted_iota(jnp.int32, (rows, LANES), 1)
    lane = lane_i.astype(F32)
    ninf = -jnp.inf
    gl = jnp.where(lane_i < N_GROUPS, logits, ninf)
    gmax = jnp.max(gl, axis=1, keepdims=True)
    gidx = jnp.min(jnp.where(gl == gmax, lane, float(LANES)), axis=1, keepdims=True)
    gsum = jnp.sum(jnp.exp(gl - gmax), axis=1, keepdims=True)
    g_p = 1.0 / gsum
    e_lane = lane_i - N_GROUPS
    lane_group = jnp.right_shift(e_lane, EXPERT_SHIFT).astype(F32)
    in_group = (e_lane >= 0) & (e_lane < N_EXPERTS) & (lane_group == gidx)
    el = jnp.where(in_group, logits, ninf)
    v1 = jnp.max(el, axis=1, keepdims=True)
    i1 = jnp.min(jnp.where(el == v1, lane, float(LANES)), axis=1, keepdims=True)
    el2 = jnp.where(lane == i1, ninf, el)
    v2 = jnp.max(el2, axis=1, keepdims=True)
    i2 = jnp.min(jnp.where(el2 == v2, lane, float(LANES)), axis=1, keepdims=True)
    t = jnp.exp(v2 - v1)
    w1 = g_p / (1.0 + t)
    w2 = g_p * t / (1.0 + t)
    @pl.when(pl.program_id(0) == 0)
    def _():
        cnt_ref[...] = jnp.zeros_like(cnt_ref)
    member = jnp.where((lane == i1) | (lane == i2), 1.0, 0.0)
    r_i = lax.broadcasted_iota(jnp.int32, (rows, rows), 0)
    c_i = lax.broadcasted_iota(jnp.int32, (rows, rows), 1)
    earlier = jnp.where(r_i > c_i, 1.0, 0.0).astype(BF16)
    before = cnt_ref[...] + jnp.dot(earlier, member.astype(BF16), preferred_element_type=F32)
    rank1 = jnp.sum(jnp.where(lane == i1, before, 0.0), axis=1, keepdims=True)
    rank2 = jnp.sum(jnp.where(lane == i2, before, 0.0), axis=1, keepdims=True)
    cnt_ref[...] = cnt_ref[...] + jnp.sum(member, axis=0, keepdims=True)
    out = jnp.where(lane_i == 0, i1 - N_GROUPS,
          jnp.where(lane_i == 1, i2 - N_GROUPS,
          jnp.where(lane_i == 2, w1,
          jnp.where(lane_i == 3, w2,
          jnp.where(lane_i == 4, rank1, jnp.where(lane_i == 5, rank2, 0.0))))))
    o_ref[...] = out


def _router(h, gain, w_hi, w_lo, bias):
    n, d = h.shape
    return pl.pallas_call(
        _router_kernel, grid=(n // ROW_TILE,),
        in_specs=[pl.BlockSpec((ROW_TILE, d), lambda i: (i, 0)),
                  pl.BlockSpec((1, d), lambda i: (0, 0)),
                  pl.BlockSpec((d, LANES), lambda i: (0, 0)),
                  pl.BlockSpec((d, LANES), lambda i: (0, 0)),
                  pl.BlockSpec((1, LANES), lambda i: (0, 0))],
        out_specs=[pl.BlockSpec((ROW_TILE, LANES), lambda i: (i, 0)),
                   pl.BlockSpec((1, LANES), lambda i: (0, 0))],
        out_shape=[jax.ShapeDtypeStruct((n, LANES), F32),
                   jax.ShapeDtypeStruct((1, LANES), F32)],
        compiler_params=_cparams(1), name="moe_router",
    )(h, gain, w_hi, w_lo, bias)


def _expert_kernel(te_ref, tok_hbm, dst_hbm, h_hbm, g_ref, wg_ref, wu_ref, wd_ref, y_hbm,
                   xbuf0, xbuf1, ybuf0, ybuf1, tok0, tok1, dst0, dst1, isem, gsem, ssem,
                   *, n_tiles):
    t = pl.program_id(0)
    R = EXPERT_TILE
    G = R // SUBLANES
    xbufs, ybufs, toks, dsts = (xbuf0, xbuf1), (ybuf0, ybuf1), (tok0, tok1), (dst0, dst1)

    def idx_copies(tile, s):
        return (pltpu.make_async_copy(tok_hbm.at[tile], toks[s], isem.at[s]),
                pltpu.make_async_copy(dst_hbm.at[tile], dsts[s], isem.at[s]))

    def start_gather(s):
        def body(g, c):
            base = pl.multiple_of(g * SUBLANES, SUBLANES)
            for u in range(SUBLANES):
                row = toks[s][base + u]
                src = h_hbm.at[lax.shift_right_logical(row, SUBLANE_SHIFT),
                               pl.ds(row & (SUBLANES - 1), 1)]
                pltpu.make_async_copy(src, xbufs[s].at[g, pl.ds(u, 1)], gsem.at[s]).start(
                    priority=u % DMA_PRIORITIES)
            return c
        lax.fori_loop(0, G, body, 0)

    def start_scatter(s):
        def body(g, c):
            base = pl.multiple_of(g * SUBLANES, SUBLANES)
            for u in range(SUBLANES):
                row = dsts[s][base + u]
                dst = y_hbm.at[lax.shift_right_logical(row, SUBLANE_SHIFT),
                               pl.ds(row & (SUBLANES - 1), 1)]
                pltpu.make_async_copy(ybufs[s].at[g, pl.ds(u, 1)], dst, ssem.at[s]).start(
                    priority=u % DMA_PRIORITIES)
            return c
        lax.fori_loop(0, G, body, 0)

    def wait_gather(s):
        pltpu.make_async_copy(h_hbm.at[pl.ds(0, G)], xbufs[s], gsem.at[s]).wait()

    def wait_scatter(s):
        pltpu.make_async_copy(ybufs[s], y_hbm.at[pl.ds(0, G)], ssem.at[s]).wait()

    def step(s):
        @pl.when(t == 0)
        def _():
            for cp in idx_copies(0, 0):
                cp.start()
            for cp in idx_copies(0, 0):
                cp.wait()
            start_gather(0)
            if n_tiles > 1:
                for cp in idx_copies(1, 1):
                    cp.start()

        @pl.when(t + 1 < n_tiles)
        def _():
            for cp in idx_copies(t + 1, 1 - s):
                cp.wait()
            start_gather(1 - s)

        wait_gather(s)

        @pl.when(t >= 2)
        def _():
            wait_scatter(s)

        x = xbufs[s][...].reshape(R, -1)
        xn = _rms(x, g_ref[...]).astype(BF16)
        a = jnp.dot(xn, wg_ref[0, 0].astype(BF16), preferred_element_type=F32)
        u = jnp.dot(xn, wu_ref[0, 0].astype(BF16), preferred_element_type=F32)
        hid = (a / (1.0 + jnp.exp(-a))) * u
        y = jnp.dot(hid.astype(BF16), wd_ref[0, 0].astype(BF16), preferred_element_type=F32)
        ybufs[s][...] = y.reshape(G, SUBLANES, -1)

        start_scatter(s)

        @pl.when(t + 2 < n_tiles)
        def _():
            for cp in idx_copies(t + 2, s):
                cp.start()

        @pl.when(t == n_tiles - 1)
        def _():
            if n_tiles > 1:
                wait_scatter(1 - s)
            wait_scatter(s)

    for s in range(2):
        pl.when(lax.rem(t, 2) == s)(functools.partial(step, s))


def _expert_mlp(tile_expert, slot_token, slot_dest, h, gain, wg, wu, wd, layer, n_out_rows):
    n_tiles = slot_token.shape[0]
    n, d = h.shape
    f = wg.shape[3]
    R = EXPERT_TILE
    G = R // SUBLANES
    kern = functools.partial(_expert_kernel, n_tiles=n_tiles)
    anyspec = pl.BlockSpec(memory_space=pl.ANY)
    buf = pltpu.VMEM((G, SUBLANES, d), F32)
    idx = pltpu.SMEM((R,), jnp.int32)
    sems = pltpu.SemaphoreType.DMA((2,))
    y = pl.pallas_call(
        kern,
        grid_spec=pltpu.PrefetchScalarGridSpec(
            num_scalar_prefetch=1, grid=(n_tiles,),
            in_specs=[anyspec, anyspec, anyspec,
                      pl.BlockSpec((1, d), lambda t, te: (0, 0)),
                      pl.BlockSpec((1, 1, d, f), lambda t, te: (layer, te[t], 0, 0)),
                      pl.BlockSpec((1, 1, d, f), lambda t, te: (layer, te[t], 0, 0)),
                      pl.BlockSpec((1, 1, f, d), lambda t, te: (layer, te[t], 0, 0))],
            out_specs=anyspec,
            scratch_shapes=[buf, buf, buf, buf, idx, idx, idx, idx, sems, sems, sems]),
        out_shape=jax.ShapeDtypeStruct((n_out_rows // SUBLANES, SUBLANES, d), F32),
        compiler_params=_cparams(1), name="moe_expert",
    )(tile_expert, slot_token, slot_dest, h.reshape(n // SUBLANES, SUBLANES, d), gain, wg, wu, wd)
    return y.reshape(n_out_rows, d)


def _combine_kernel(ya_ref, yb_ref, h_ref, r_ref, o_ref):
    route = r_ref[...]
    o_ref[...] = h_ref[...] + route[:, 2:3] * ya_ref[...] + route[:, 3:4] * yb_ref[...]


def _combine(y, h, route):
    n, d = h.shape
    steps = n // ROW_TILE
    row = lambda off: pl.BlockSpec((ROW_TILE, d), lambda i: (off + i, 0))
    return pl.pallas_call(
        _combine_kernel, grid=(steps,),
        in_specs=[row(0), row(steps), row(0), pl.BlockSpec((ROW_TILE, LANES), lambda i: (i, 0))],
        out_specs=row(0), out_shape=jax.ShapeDtypeStruct((n, d), F32),
        compiler_params=_cparams(1), name="moe_combine",
    )(y, y, h, route)


def _moe(h, gain, w_group, b_group, w_router, b_router, w_gate, w_up, w_down, layer):
    n, d = h.shape
    pad = LANES - N_GROUPS - N_EXPERTS
    w_cat = jnp.concatenate([w_group, w_router, jnp.zeros((d, pad), F32)], axis=1)
    b_cat = jnp.concatenate([b_group, b_router, jnp.zeros((pad,), F32)])[None, :]
    w_hi = w_cat.astype(BF16)
    w_lo = (w_cat - w_hi.astype(F32)).astype(BF16)
    route, lane_counts = _router(h, gain, w_hi, w_lo, b_cat)

    e_flat = route[:, 0:2].astype(jnp.int32).reshape(-1)
    rank = route[:, 4:6].astype(jnp.int32).reshape(-1)
    counts = lane_counts[0, N_GROUPS:N_GROUPS + N_EXPERTS].astype(jnp.int32)
    ptiles = (counts + EXPERT_TILE - 1) // EXPERT_TILE
    tile_end = jnp.cumsum(ptiles)
    pstart = (tile_end - ptiles) * EXPERT_TILE
    pos = pstart[e_flat] + rank
    n_tiles = (2 * n) // EXPERT_TILE + N_EXPERTS
    n_slots = n_tiles * EXPERT_TILE
    slot_pair = jnp.full((n_slots,), -1, jnp.int32).at[pos].set(
        jnp.arange(2 * n, dtype=jnp.int32))
    is_pad = slot_pair < 0
    slot_token = jnp.where(is_pad, 0, slot_pair // 2)
    slot_dest = jnp.where(is_pad, 2 * n - 1 + jnp.cumsum(is_pad.astype(jnp.int32)),
                          (slot_pair % 2) * n + slot_pair // 2)
    tile_ids = jnp.arange(n_tiles, dtype=jnp.int32)
    tile_expert = jnp.minimum(
        jnp.sum((tile_ids[:, None] >= tile_end[None, :]).astype(jnp.int32), axis=1),
        N_EXPERTS - 1).astype(jnp.int32)
    n_valid = tile_end[-1]
    last_expert = tile_expert[jnp.maximum(n_valid - 1, 0)]
    tile_expert = jnp.where(tile_ids < n_valid, tile_expert, last_expert)

    y = _expert_mlp(tile_expert, slot_token.reshape(n_tiles, EXPERT_TILE),
                    slot_dest.reshape(n_tiles, EXPERT_TILE), h, gain,
                    w_gate, w_up, w_down, layer, n_slots)
    return _combine(y, h, route)


def kernel(x, meta_tokens, norm_mix, norm_ffn, diff_w_qkv, diff_q_gain, diff_k_gain,
           diff_lambda_q1, diff_lambda_k1, diff_lambda_q2, diff_lambda_k2, diff_sub_gain,
           diff_w_o, sb_w_qkv, sb_w_o, moe_w_group, moe_b_group, moe_w_router,
           moe_b_router, moe_w_gate, moe_w_up, moe_w_down):
    B, S, D = x.shape
    L = N_META + S
    Lp = -(-L // TQ) * TQ
    depth = norm_mix.shape[0]
    assert D == D_MODEL and TQ % Q_BLOCK == 0 and TQ % TK_MID == 0 and TK_BIG == TQ
    assert (B * Lp) % ROW_TILE == 0
    assert (2 * B * Lp) % EXPERT_TILE == 0

    meta = jnp.broadcast_to(meta_tokens.astype(x.dtype)[None], (B, N_META, D))
    h = jnp.concatenate([meta, x, jnp.zeros((B, Lp - L, D), x.dtype)], axis=1).reshape(B * Lp, D)

    blockdiag = (jnp.arange(MXU_DIM)[:, None] // DIFF_HEAD_DIM
                 == jnp.arange(MXU_DIM)[None, :] // DIFF_HEAD_DIM).astype(BF16)
    slopes = jnp.exp2(-8.0 * (jnp.arange(DIFF_HEADS, dtype=F32) + 1.0) / DIFF_HEADS)

    for i in range(depth):
        j = i // N_MIXERS
        gain = norm_mix[i][None, :]
        if i % N_MIXERS == 0:
            lambda_init = 0.8 - 0.6 * math.exp(-0.3 * i)
            scale = DIFF_HEAD_DIM ** -0.5 * LOG2E
            reps = D // DIFF_HEAD_DIM
            qg = (jnp.tile(diff_q_gain[j], reps) * scale)[None, :]
            kg = jnp.tile(diff_k_gain[j], reps)[None, :]
            qkv = _norm_qkv(h, gain, diff_w_qkv[j].astype(BF16), (qg, kg, blockdiag))
            lam_params = jnp.stack([diff_lambda_q1[j], diff_lambda_k1[j],
                                    diff_lambda_q2[j], diff_lambda_k2[j]])
            o = _diff_attention(qkv, slopes, lam_params, diff_sub_gain[j][None, :], B, Lp,
                                lambda_init)
            h = _proj_residual(o, diff_w_o[j].astype(BF16), h)
        else:
            scale = SB_HEAD_DIM ** -0.5 * LOG2E
            colscale = jnp.concatenate([jnp.full((D,), scale, F32), jnp.ones((2 * D,), F32)])
            w = (sb_w_qkv[j] * colscale[None, :]).astype(BF16)
            qkv = _norm_qkv(h, gain, w)
            o = _sb_attention(qkv, B, Lp)
            h = _proj_residual(o, sb_w_o[j].astype(BF16), h)
        h = _moe(h, norm_ffn[i][None, :], moe_w_group[i], moe_b_group[i], moe_w_router[i],
                 moe_b_router[i], moe_w_gate, moe_w_up, moe_w_down, i)
    return h.reshape(B, Lp, D)[:, N_META:N_META + S]
```

```python
import functools
import math

import jax
import jax.numpy as jnp
from jax import lax
from jax.experimental import pallas as pl
from jax.experimental.pallas import tpu as pltpu

F32 = jnp.float32
BF16 = jnp.bfloat16

D_MODEL = 1024
CHUNK = 64
CHUNK_SHIFT = 6
N_META = 16
Q_BLOCK = 128
N_MIXERS = 2
DIFF_HEADS = 8
DIFF_HEAD_DIM = D_MODEL // (2 * DIFF_HEADS)
SB_HEADS = 16
SB_HEAD_DIM = D_MODEL // SB_HEADS
N_GROUPS = 4
EXPERTS_PER_GROUP = 8
EXPERT_SHIFT = 3
N_EXPERTS = N_GROUPS * EXPERTS_PER_GROUP
D_EXPERT = 512
NORM_EPS = 1e-6

LANES = 128
MXU_DIM = 256
VMEM_LIMIT = 56 * 1024 * 1024

ROW_TILE = 512
TQ = 768
TK_BIG = TQ
TK_MID = MXU_DIM
TK_SMALL = 128
ROW_BLOCK = 384
LOG2E = math.log2(math.e)
SOFTPLUS_CAP = 64.0
EXPERT_TILE = 256
DMA_PRIORITIES = 2
DMA_UNROLL = 8
NEG_BIG = -1e30


def _cparams(n_axes):
    return pltpu.CompilerParams(
        dimension_semantics=("arbitrary",) * n_axes,
        vmem_limit_bytes=VMEM_LIMIT,
    )


def _rms(x, gain):
    ms = jnp.mean(x * x, axis=-1, keepdims=True)
    return x * lax.rsqrt(ms + NORM_EPS) * gain


def _norm_qkv_diff_kernel(h_ref, g_ref, w_ref, qg_ref, kg_ref, gm_ref, o_ref):
    D = D_MODEL
    xn = _rms(h_ref[...], g_ref[...]).astype(BF16)
    for part, gain_ref in ((0, qg_ref), (1, kg_ref)):
        y = jnp.dot(xn, w_ref[:, part * D:(part + 1) * D], preferred_element_type=F32)
        y2 = (y * y).astype(BF16)
        ss = jnp.concatenate(
            [jnp.dot(y2[:, c * MXU_DIM:(c + 1) * MXU_DIM], gm_ref[...],
                     preferred_element_type=F32) for c in range(D // MXU_DIM)], axis=1)
        yn = y * lax.rsqrt(ss * (1.0 / DIFF_HEAD_DIM) + NORM_EPS) * gain_ref[...]
        o_ref[:, part * D:(part + 1) * D] = yn.astype(BF16)
    v = jnp.dot(xn, w_ref[:, 2 * D:3 * D], preferred_element_type=F32)
    o_ref[:, 2 * D:3 * D] = v.astype(BF16)


def _norm_qkv_sb_kernel(h_ref, g_ref, w_ref, o_ref):
    D = D_MODEL
    xn = _rms(h_ref[...], g_ref[...]).astype(BF16)
    for part in range(3):
        y = jnp.dot(xn, w_ref[:, part * D:(part + 1) * D], preferred_element_type=F32)
        o_ref[:, part * D:(part + 1) * D] = y.astype(BF16)


def _norm_qkv(h, gain, w_bf16, qk=None):
    n, d = h.shape
    grid = (n // ROW_TILE,)
    row = pl.BlockSpec((ROW_TILE, d), lambda i: (i, 0))
    vec = pl.BlockSpec((1, d), lambda i: (0, 0))
    wspec = pl.BlockSpec((d, 3 * d), lambda i: (0, 0))
    out = pl.BlockSpec((ROW_TILE, 3 * d), lambda i: (i, 0))
    out_shape = jax.ShapeDtypeStruct((n, 3 * d), BF16)
    if qk is None:
        return pl.pallas_call(
            _norm_qkv_sb_kernel, grid=grid, in_specs=[row, vec, wspec], out_specs=out,
            out_shape=out_shape, compiler_params=_cparams(1), name="norm_qkv_sb",
        )(h, gain, w_bf16)
    qg, kg, gm = qk
    gspec = pl.BlockSpec((MXU_DIM, MXU_DIM), lambda i: (0, 0))
    return pl.pallas_call(
        _norm_qkv_diff_kernel, grid=grid, in_specs=[row, vec, wspec, vec, vec, gspec],
        out_specs=out, out_shape=out_shape, compiler_params=_cparams(1), name="norm_qkv_diff",
    )(h, gain, w_bf16, qg, kg, gm)


def _diff_attn_kernel(slopes_ref, q_ref, k_ref, v_ref, lam_ref, sg_ref, o_ref,
                      acc0, acc1, m0, m1, *, seq, lambda_init):
    hd = pl.program_id(1)
    slope = slopes_ref[hd] * LOG2E
    accs, ms = (acc0, acc1), (m0, m1)
    lp = lam_ref[...]
    lam = (jnp.exp(jnp.sum(lp[0:1] * lp[1:2], axis=-1, keepdims=True))
           - jnp.exp(jnp.sum(lp[2:3] * lp[3:4], axis=-1, keepdims=True)) + lambda_init)
    ones = jnp.ones((TK_BIG, LANES), BF16)
    lane = lax.broadcasted_iota(jnp.int32, (TQ, LANES), 1)
    dn = (((1,), (1,)), ((), ()))

    def q_body(qi, carry):
        qs = pl.multiple_of(qi * TQ, TQ)
        q = q_ref[pl.ds(qs, TQ), :]
        zero = jnp.zeros_like(q)
        qc = (jnp.where(lane < DIFF_HEAD_DIM, q, zero), jnp.where(lane >= DIFF_HEAD_DIM, q, zero))
        qpos = qs + lax.broadcasted_iota(jnp.int32, (TQ, 1), 0)
        bound = ((qpos + (CHUNK - N_META)) >> CHUNK_SHIFT) * CHUNK + N_META
        for acc, m in ((acc0, m0), (acc1, m1)):
            acc[...] = jnp.zeros_like(acc)
            m[...] = jnp.full_like(m, NEG_BIG)

        def tile(kstart, tk, masked, row0=0):
            k = k_ref[pl.ds(kstart, tk), :]
            v = v_ref[pl.ds(kstart, tk), :]
            vext = jnp.concatenate([v, ones[:tk]], axis=1)
            kpos = kstart + lax.broadcasted_iota(jnp.int32, (1, tk), 1)
            rel = (kpos - qs).astype(F32) * slope
            for r0 in range(row0, TQ, ROW_BLOCK):
                r1 = min(r0 + ROW_BLOCK, TQ)
                rows = pl.ds(r0, r1 - r0)
                if masked:
                    ahead = jnp.maximum(kpos - qpos[r0:r1], 0).astype(F32) * (2.0 * slope)
                    allowed = kpos < bound[r0:r1]
                for c in range(2):
                    acc, m = accs[c], ms[c]
                    s = lax.dot_general(qc[c][r0:r1], k, dn, preferred_element_type=F32) + rel
                    if masked:
                        s = jnp.where(allowed, s - ahead, -jnp.inf)
                    m_old = m[rows, :]
                    m_new = jnp.maximum(m_old, jnp.max(s, axis=1, keepdims=True))
                    alpha = jnp.exp2(m_old - m_new)
                    p = jnp.exp2(s - m_new).astype(BF16)
                    acc[rows, :] = (acc[rows, :] * alpha
                                    + jnp.dot(p, vext, preferred_element_type=F32))
                    m[rows, :] = m_new

        n_big = qi
        def big_body(j, c):
            tile(pl.multiple_of(2 * j * TK_BIG, TK_BIG), TK_BIG, False)
            tile(pl.multiple_of((2 * j + 1) * TK_BIG, TK_BIG), TK_BIG, False)
            return c
        lax.fori_loop(0, n_big // 2, big_body, 0)
        @pl.when(n_big % 2 == 1)
        def _():
            tile(pl.multiple_of((n_big - 1) * TK_BIG, TK_BIG), TK_BIG, False)
        tile(qs, TQ, True)
        @pl.when(qs + TQ < seq)
        def _():
            tile(pl.multiple_of(qs + TQ, TK_SMALL), TK_SMALL, True, TQ - CHUNK)

        a0 = acc0[...]
        a1 = acc1[...]
        o = a0[:, :LANES] / a0[:, LANES:] - lam * (a1[:, :LANES] / a1[:, LANES:])
        o = _rms(o, sg_ref[...]) * (1.0 - lambda_init)
        o_ref[pl.ds(qs, TQ), :] = o.astype(o_ref.dtype)
        return carry

    lax.fori_loop(0, seq // TQ, q_body, 0)


def _diff_attention(qkv, slopes, lam_params, sub_gain, batch, seq, lambda_init):
    H = DIFF_HEADS
    kern = functools.partial(_diff_attn_kernel, seq=seq, lambda_init=lambda_init)
    blk = lambda off: pl.BlockSpec((seq, LANES), lambda b, h, s: (b, off + h))
    return pl.pallas_call(
        kern,
        grid_spec=pltpu.PrefetchScalarGridSpec(
            num_scalar_prefetch=1, grid=(batch, H),
            in_specs=[blk(0), blk(H), blk(2 * H),
                      pl.BlockSpec((4, DIFF_HEAD_DIM), lambda b, h, s: (0, 0)),
                      pl.BlockSpec((1, LANES), lambda b, h, s: (0, 0))],
            out_specs=pl.BlockSpec((seq, LANES), lambda b, h, s: (b, h)),
            scratch_shapes=[pltpu.VMEM((TQ, 2 * LANES), F32), pltpu.VMEM((TQ, 2 * LANES), F32),
                            pltpu.VMEM((TQ, 1), F32), pltpu.VMEM((TQ, 1), F32)]),
        out_shape=jax.ShapeDtypeStruct((batch * seq, D_MODEL), BF16),
        compiler_params=_cparams(2), name="diff_attn",
    )(slopes, qkv, qkv, qkv, lam_params, sub_gain)


def _sb_attn_kernel(q_ref, k_ref, v_ref, o_ref, acc, run0, run1, *, seq):
    d = SB_HEAD_DIM
    lane = lax.broadcasted_iota(jnp.int32, (TQ, LANES), 1)
    dn = (((1,), (1,)), ((), ()))

    tk = TK_MID
    r_i = lax.broadcasted_iota(jnp.int32, (tk, tk), 0)
    c_i = lax.broadcasted_iota(jnp.int32, (tk, tk), 1)
    from_here = jnp.where(r_i >= c_i, 1.0, 0.0).astype(BF16)
    vlow = lax.broadcasted_iota(jnp.int32, (tk, LANES), 1) < d
    runs = (run0, run1)

    def q_body(qi, carry):
        qs = pl.multiple_of(qi * TQ, TQ)
        q = q_ref[pl.ds(qs, TQ), :]
        zero = jnp.zeros_like(q)
        qh = (jnp.where(lane < d, q, zero), jnp.where(lane >= d, q, zero))
        qpos = qs + lax.broadcasted_iota(jnp.int32, (TQ, 1), 0)
        acc[...] = jnp.zeros_like(acc)
        run0[...] = jnp.zeros_like(run0)
        run1[...] = jnp.zeros_like(run1)

        def tile(kstart, masked, row0=0):
            k = k_ref[pl.ds(kstart, tk), :]
            v = v_ref[pl.ds(kstart, tk), :]
            vzero = jnp.zeros_like(v)
            vh = (jnp.where(vlow, v, vzero), jnp.where(vlow, vzero, v))
            kpos = kstart + lax.broadcasted_iota(jnp.int32, (1, tk), 1)
            for r0 in range(row0, TQ, ROW_BLOCK):
                r1 = min(r0 + ROW_BLOCK, TQ)
                rows = pl.ds(r0, r1 - r0)
                if masked:
                    before = kpos < qpos[r0:r1]
                total = acc[rows, :]
                for hh in range(2):
                    run = runs[hh]
                    z = lax.dot_general(qh[hh][r0:r1], k, dn, preferred_element_type=F32)
                    drop = jnp.maximum(z, jnp.log2(1.0 + jnp.exp2(jnp.minimum(z, SOFTPLUS_CAP))))
                    if masked:
                        drop = jnp.where(before, drop, 0.0)
                    tail = jnp.dot(drop.astype(BF16), from_here, preferred_element_type=F32)
                    r_old = run[rows, :]
                    w = jnp.exp2((z - tail) - r_old)
                    if masked:
                        w = jnp.where(before, w, 0.0)
                    run[rows, :] = r_old + tail[:, 0:1]
                    total = total + jnp.dot(w.astype(BF16), vh[hh], preferred_element_type=F32)
                acc[rows, :] = total

        for c in reversed(range(TQ // tk)):
            tile(pl.multiple_of(qs + c * tk, tk), True, c * tk)
        n_full = qs // tk
        per = TQ // tk
        def full_body(j, c):
            for i in range(per):
                tile(pl.multiple_of((n_full - 1 - i - per * j) * tk, tk), False)
            return c
        lax.fori_loop(0, n_full // per, full_body, 0)
        o_ref[pl.ds(qs, TQ), :] = acc[...].astype(o_ref.dtype)
        return carry

    lax.fori_loop(0, seq // TQ, q_body, 0)


def _sb_attention(qkv, batch, seq):
    P = SB_HEADS // 2
    kern = functools.partial(_sb_attn_kernel, seq=seq)
    blk = lambda off: pl.BlockSpec((seq, LANES), lambda b, p: (b, off + p))
    return pl.pallas_call(
        kern, grid=(batch, P),
        in_specs=[blk(0), blk(P), blk(2 * P)],
        out_specs=pl.BlockSpec((seq, LANES), lambda b, p: (b, p)),
        scratch_shapes=[pltpu.VMEM((TQ, LANES), F32), pltpu.VMEM((TQ, 1), F32),
                        pltpu.VMEM((TQ, 1), F32)],
        out_shape=jax.ShapeDtypeStruct((batch * seq, D_MODEL), BF16),
        compiler_params=_cparams(2), name="sb_attn",
    )(qkv, qkv, qkv)


def _to_row_tiles(ref, x):
    per = x.shape[1] // LANES
    for j in range(per):
        ref[pl.ds(j, x.shape[0], stride=per), :] = x[:, j * LANES:(j + 1) * LANES]


def _from_row_tiles(ref, rows, per):
    return jnp.concatenate([ref[pl.ds(j, rows, stride=per), :] for j in range(per)], axis=1)


def _proj_res_kernel(o_ref, w_ref, h_ref, out_ref, tiles_ref):
    out = h_ref[...] + jnp.dot(o_ref[...], w_ref[...], preferred_element_type=F32)
    out_ref[...] = out
    _to_row_tiles(tiles_ref, out)


def _proj_residual(o, w_bf16, h):
    n, d = h.shape
    per = d // LANES
    row = lambda: pl.BlockSpec((ROW_TILE, d), lambda i: (i, 0))
    return pl.pallas_call(
        _proj_res_kernel, grid=(n // ROW_TILE,),
        in_specs=[row(), pl.BlockSpec((d, d), lambda i: (0, 0)), row()],
        out_specs=[row(), pl.BlockSpec((ROW_TILE * per, LANES), lambda i: (i, 0))],
        out_shape=[jax.ShapeDtypeStruct((n, d), F32),
                   jax.ShapeDtypeStruct((n * per, LANES), F32)],
        compiler_params=_cparams(1), name="proj_residual",
    )(o, w_bf16, h)


def _split3(x):
    hi = x.astype(BF16)
    return hi, (x - hi.astype(F32)).astype(BF16)


def _router_kernel(h_ref, g_ref, whi_ref, wlo_ref, b_ref, o_ref, cnt_ref):
    xn = _rms(h_ref[...], g_ref[...])
    xhi, xlo = _split3(xn)
    logits = (jnp.dot(xhi, whi_ref[...], preferred_element_type=F32)
              + jnp.dot(xhi, wlo_ref[...], preferred_element_type=F32)
              + jnp.dot(xlo, whi_ref[...], preferred_element_type=F32)) + b_ref[...]
    rows = logits.shape[0]
    lane_i = lax.broadcasted_iota(jnp.int32, (rows, LANES), 1)
    lane = lane_i.astype(F32)
    ninf = -jnp.inf
    gl = jnp.where(lane_i < N_GROUPS, logits, ninf)
    gmax = jnp.max(gl, axis=1, keepdims=True)
    gidx = jnp.min(jnp.where(gl == gmax, lane, float(LANES)), axis=1, keepdims=True)
    gsum = jnp.sum(jnp.exp(gl - gmax), axis=1, keepdims=True)
    g_p = 1.0 / gsum
    e_lane = lane_i - N_GROUPS
    lane_group = jnp.right_shift(e_lane, EXPERT_SHIFT).astype(F32)
    in_group = (e_lane >= 0) & (e_lane < N_EXPERTS) & (lane_group == gidx)
    el = jnp.where(in_group, logits, ninf)
    v1 = jnp.max(el, axis=1, keepdims=True)
    i1 = jnp.min(jnp.where(el == v1, lane, float(LANES)), axis=1, keepdims=True)
    el2 = jnp.where(lane == i1, ninf, el)
    v2 = jnp.max(el2, axis=1, keepdims=True)
    i2 = jnp.min(jnp.where(el2 == v2, lane, float(LANES)), axis=1, keepdims=True)
    t = jnp.exp(v2 - v1)
    w1 = g_p / (1.0 + t)
    w2 = g_p * t / (1.0 + t)
    @pl.when(pl.program_id(0) == 0)
    def _():
        cnt_ref[...] = jnp.zeros_like(cnt_ref)
    member = jnp.where((lane == i1) | (lane == i2), 1.0, 0.0)
    r_i = lax.broadcasted_iota(jnp.int32, (rows, rows), 0)
    c_i = lax.broadcasted_iota(jnp.int32, (rows, rows), 1)
    earlier = jnp.where(r_i > c_i, 1.0, 0.0).astype(BF16)
    before = cnt_ref[...] + jnp.dot(earlier, member.astype(BF16), preferred_element_type=F32)
    rank1 = jnp.sum(jnp.where(lane == i1, before, 0.0), axis=1, keepdims=True)
    rank2 = jnp.sum(jnp.where(lane == i2, before, 0.0), axis=1, keepdims=True)
    cnt_ref[...] = cnt_ref[...] + jnp.sum(member, axis=0, keepdims=True)
    out = jnp.where(lane_i == 0, i1 - N_GROUPS,
          jnp.where(lane_i == 1, i2 - N_GROUPS,
          jnp.where(lane_i == 2, w1,
          jnp.where(lane_i == 3, w2,
          jnp.where(lane_i == 4, rank1, jnp.where(lane_i == 5, rank2, 0.0))))))
    o_ref[...] = out


def _router(h, gain, w_hi, w_lo, bias):
    n, d = h.shape
    return pl.pallas_call(
        _router_kernel, grid=(n // ROW_TILE,),
        in_specs=[pl.BlockSpec((ROW_TILE, d), lambda i: (i, 0)),
                  pl.BlockSpec((1, d), lambda i: (0, 0)),
                  pl.BlockSpec((d, LANES), lambda i: (0, 0)),
                  pl.BlockSpec((d, LANES), lambda i: (0, 0)),
                  pl.BlockSpec((1, LANES), lambda i: (0, 0))],
        out_specs=[pl.BlockSpec((ROW_TILE, LANES), lambda i: (i, 0)),
                   pl.BlockSpec((1, LANES), lambda i: (0, 0))],
        out_shape=[jax.ShapeDtypeStruct((n, LANES), F32),
                   jax.ShapeDtypeStruct((1, LANES), F32)],
        compiler_params=_cparams(1), name="moe_router",
    )(h, gain, w_hi, w_lo, bias)


def _expert_kernel(te_ref, tok_hbm, dst_hbm, h_hbm, g_ref, wg_ref, wu_ref, wd_ref, y_hbm,
                   xbuf0, xbuf1, ybuf0, ybuf1, tok0, tok1, dst0, dst1, isem, gsem, ssem,
                   *, n_tiles):
    t = pl.program_id(0)
    R = EXPERT_TILE
    per = g_ref.shape[1] // LANES
    xbufs, ybufs, toks, dsts = (xbuf0, xbuf1), (ybuf0, ybuf1), (tok0, tok1), (dst0, dst1)

    def idx_copies(tile, s):
        return (pltpu.make_async_copy(tok_hbm.at[tile], toks[s], isem.at[s]),
                pltpu.make_async_copy(dst_hbm.at[tile], dsts[s], isem.at[s]))

    def start_gather(s):
        def body(g, c):
            for u in range(DMA_UNROLL):
                r = g * DMA_UNROLL + u
                src = h_hbm.at[pl.ds(pl.multiple_of(toks[s][r], per), per)]
                dst = xbufs[s].at[pl.ds(pl.multiple_of(r * per, per), per)]
                pltpu.make_async_copy(src, dst, gsem.at[s]).start(priority=u % DMA_PRIORITIES)
            return c
        lax.fori_loop(0, R // DMA_UNROLL, body, 0)

    def start_scatter(s):
        def body(g, c):
            for u in range(DMA_UNROLL):
                r = g * DMA_UNROLL + u
                src = ybufs[s].at[pl.ds(pl.multiple_of(r * per, per), per)]
                dst = y_hbm.at[pl.ds(pl.multiple_of(dsts[s][r], per), per)]
                pltpu.make_async_copy(src, dst, ssem.at[s]).start(priority=u % DMA_PRIORITIES)
            return c
        lax.fori_loop(0, R // DMA_UNROLL, body, 0)

    def wait_gather(s):
        pltpu.make_async_copy(h_hbm.at[pl.ds(0, R * per)], xbufs[s], gsem.at[s]).wait()

    def wait_scatter(s):
        pltpu.make_async_copy(ybufs[s], y_hbm.at[pl.ds(0, R * per)], ssem.at[s]).wait()

    def step(s):
        @pl.when(t == 0)
        def _():
            for cp in idx_copies(0, 0):
                cp.start()
            for cp in idx_copies(0, 0):
                cp.wait()
            start_gather(0)
            if n_tiles > 1:
                for cp in idx_copies(1, 1):
                    cp.start()

        @pl.when(t + 1 < n_tiles)
        def _():
            for cp in idx_copies(t + 1, 1 - s):
                cp.wait()
            start_gather(1 - s)

        wait_gather(s)

        @pl.when(t >= 2)
        def _():
            wait_scatter(s)

        x = _from_row_tiles(xbufs[s], R, per)
        xn = _rms(x, g_ref[...]).astype(BF16)
        a = jnp.dot(xn, wg_ref[0, 0].astype(BF16), preferred_element_type=F32)
        u = jnp.dot(xn, wu_ref[0, 0].astype(BF16), preferred_element_type=F32)
        hid = (a / (1.0 + jnp.exp(-a))) * u
        y = jnp.dot(hid.astype(BF16), wd_ref[0, 0].astype(BF16), preferred_element_type=F32)
        _to_row_tiles(ybufs[s], y)

        start_scatter(s)

        @pl.when(t + 2 < n_tiles)
        def _():
            for cp in idx_copies(t + 2, s):
                cp.start()

        @pl.when(t == n_tiles - 1)
        def _():
            if n_tiles > 1:
                wait_scatter(1 - s)
            wait_scatter(s)

    for s in range(2):
        pl.when(lax.rem(t, 2) == s)(functools.partial(step, s))


def _expert_mlp(tile_expert, slot_token, slot_dest, h_tiles, gain, wg, wu, wd, layer, n_out_rows):
    n_tiles = slot_token.shape[0]
    d = gain.shape[1]
    per = d // LANES
    f = wg.shape[3]
    R = EXPERT_TILE
    kern = functools.partial(_expert_kernel, n_tiles=n_tiles)
    anyspec = pl.BlockSpec(memory_space=pl.ANY)
    buf = pltpu.VMEM((R * per, LANES), F32)
    idx = pltpu.SMEM((R,), jnp.int32)
    sems = pltpu.SemaphoreType.DMA((2,))
    return pl.pallas_call(
        kern,
        grid_spec=pltpu.PrefetchScalarGridSpec(
            num_scalar_prefetch=1, grid=(n_tiles,),
            in_specs=[anyspec, anyspec, anyspec,
                      pl.BlockSpec((1, d), lambda t, te: (0, 0)),
                      pl.BlockSpec((1, 1, d, f), lambda t, te: (layer, te[t], 0, 0)),
                      pl.BlockSpec((1, 1, d, f), lambda t, te: (layer, te[t], 0, 0)),
                      pl.BlockSpec((1, 1, f, d), lambda t, te: (layer, te[t], 0, 0))],
            out_specs=anyspec,
            scratch_shapes=[buf, buf, buf, buf, idx, idx, idx, idx, sems, sems, sems]),
        out_shape=jax.ShapeDtypeStruct((n_out_rows * per, LANES), F32),
        compiler_params=_cparams(1), name="moe_expert",
    )(tile_expert, slot_token, slot_dest, h_tiles, gain, wg, wu, wd)


def _combine_kernel(ya_ref, yb_ref, h_ref, r_ref, o_ref):
    rows, d = h_ref.shape
    per = d // LANES
    route = r_ref[...]
    o_ref[...] = (h_ref[...] + route[:, 2:3] * _from_row_tiles(ya_ref, rows, per)
                  + route[:, 3:4] * _from_row_tiles(yb_ref, rows, per))


def _combine(y_tiles, h, route):
    n, d = h.shape
    per = d // LANES
    steps = n // ROW_TILE
    row = pl.BlockSpec((ROW_TILE, d), lambda i: (i, 0))
    tiles = lambda off: pl.BlockSpec((ROW_TILE * per, LANES), lambda i: (off + i, 0))
    return pl.pallas_call(
        _combine_kernel, grid=(steps,),
        in_specs=[tiles(0), tiles(steps), row, pl.BlockSpec((ROW_TILE, LANES), lambda i: (i, 0))],
        out_specs=row, out_shape=jax.ShapeDtypeStruct((n, d), F32),
        compiler_params=_cparams(1), name="moe_combine",
    )(y_tiles, y_tiles, h, route)


def _moe(h, h_tiles, gain, w_group, b_group, w_router, b_router, w_gate, w_up, w_down, layer):
    n, d = h.shape
    per = d // LANES
    pad = LANES - N_GROUPS - N_EXPERTS
    w_cat = jnp.concatenate([w_group, w_router, jnp.zeros((d, pad), F32)], axis=1)
    b_cat = jnp.concatenate([b_group, b_router, jnp.zeros((pad,), F32)])[None, :]
    w_hi = w_cat.astype(BF16)
    w_lo = (w_cat - w_hi.astype(F32)).astype(BF16)
    route, lane_counts = _router(h, gain, w_hi, w_lo, b_cat)

    e_flat = route[:, 0:2].astype(jnp.int32).reshape(-1)
    rank = route[:, 4:6].astype(jnp.int32).reshape(-1)
    counts = lane_counts[0, N_GROUPS:N_GROUPS + N_EXPERTS].astype(jnp.int32)
    ptiles = (counts + EXPERT_TILE - 1) // EXPERT_TILE
    tile_end = jnp.cumsum(ptiles)
    pstart = (tile_end - ptiles) * EXPERT_TILE
    pos = pstart[e_flat] + rank
    n_tiles = (2 * n) // EXPERT_TILE + N_EXPERTS
    n_slots = n_tiles * EXPERT_TILE
    slot_pair = jnp.full((n_slots,), -1, jnp.int32).at[pos].set(
        jnp.arange(2 * n, dtype=jnp.int32))
    is_pad = slot_pair < 0
    slot_token = jnp.where(is_pad, 0, slot_pair // 2) * per
    slot_dest = jnp.where(is_pad, 2 * n - 1 + jnp.cumsum(is_pad.astype(jnp.int32)),
                          (slot_pair % 2) * n + slot_pair // 2) * per
    tile_ids = jnp.arange(n_tiles, dtype=jnp.int32)
    tile_expert = jnp.minimum(
        jnp.sum((tile_ids[:, None] >= tile_end[None, :]).astype(jnp.int32), axis=1),
        N_EXPERTS - 1).astype(jnp.int32)
    n_valid = tile_end[-1]
    last_expert = tile_expert[jnp.maximum(n_valid - 1, 0)]
    tile_expert = jnp.where(tile_ids < n_valid, tile_expert, last_expert)

    y = _expert_mlp(tile_expert, slot_token.reshape(n_tiles, EXPERT_TILE),
                    slot_dest.reshape(n_tiles, EXPERT_TILE), h_tiles, gain,
                    w_gate, w_up, w_down, layer, n_slots)
    return _combine(y, h, route)


def kernel(x, meta_tokens, norm_mix, norm_ffn, diff_w_qkv, diff_q_gain, diff_k_gain,
           diff_lambda_q1, diff_lambda_k1, diff_lambda_q2, diff_lambda_k2, diff_sub_gain,
           diff_w_o, sb_w_qkv, sb_w_o, moe_w_group, moe_b_group, moe_w_router,
           moe_b_router, moe_w_gate, moe_w_up, moe_w_down):
    B, S, D = x.shape
    L = N_META + S
    Lp = -(-L // TQ) * TQ
    depth = norm_mix.shape[0]
    assert D == D_MODEL and TQ % Q_BLOCK == 0 and TQ % TK_MID == 0 and TK_BIG == TQ
    assert (B * Lp) % ROW_TILE == 0
    assert (2 * B * Lp) % EXPERT_TILE == 0

    meta = jnp.broadcast_to(meta_tokens.astype(x.dtype)[None], (B, N_META, D))
    h = jnp.concatenate([meta, x, jnp.zeros((B, Lp - L, D), x.dtype)], axis=1).reshape(B * Lp, D)

    blockdiag = (jnp.arange(MXU_DIM)[:, None] // DIFF_HEAD_DIM
                 == jnp.arange(MXU_DIM)[None, :] // DIFF_HEAD_DIM).astype(BF16)
    slopes = jnp.exp2(-8.0 * (jnp.arange(DIFF_HEADS, dtype=F32) + 1.0) / DIFF_HEADS)

    for i in range(depth):
        j = i // N_MIXERS
        gain = norm_mix[i][None, :]
        if i % N_MIXERS == 0:
            lambda_init = 0.8 - 0.6 * math.exp(-0.3 * i)
            scale = DIFF_HEAD_DIM ** -0.5 * LOG2E
            reps = D // DIFF_HEAD_DIM
            qg = (jnp.tile(diff_q_gain[j], reps) * scale)[None, :]
            kg = jnp.tile(diff_k_gain[j], reps)[None, :]
            qkv = _norm_qkv(h, gain, diff_w_qkv[j].astype(BF16), (qg, kg, blockdiag))
            lam_params = jnp.stack([diff_lambda_q1[j], diff_lambda_k1[j],
                                    diff_lambda_q2[j], diff_lambda_k2[j]])
            o = _diff_attention(qkv, slopes, lam_params, diff_sub_gain[j][None, :], B, Lp,
                                lambda_init)
            h, h_tiles = _proj_residual(o, diff_w_o[j].astype(BF16), h)
        else:
            scale = SB_HEAD_DIM ** -0.5 * LOG2E
            colscale = jnp.concatenate([jnp.full((D,), scale, F32), jnp.ones((2 * D,), F32)])
            w = (sb_w_qkv[j] * colscale[None, :]).astype(BF16)
            qkv = _norm_qkv(h, gain, w)
            o = _sb_attention(qkv, B, Lp)
            h, h_tiles = _proj_residual(o, sb_w_o[j].astype(BF16), h)
        h = _moe(h, h_tiles, norm_ffn[i][None, :], moe_w_group[i], moe_b_group[i], moe_w_router[i],
                 moe_b_router[i], moe_w_gate, moe_w_up, moe_w_down, i)
    return h.reshape(B, Lp, D)[:, N_META:N_META + S]
```

```python
import functools
import math

import jax
import jax.numpy as jnp
from jax import lax
from jax.experimental import pallas as pl
from jax.experimental.pallas import tpu as pltpu

F32 = jnp.float32
BF16 = jnp.bfloat16

D_MODEL = 1024
CHUNK = 64
CHUNK_SHIFT = 6
N_META = 16
Q_BLOCK = 128
N_MIXERS = 2
DIFF_HEADS = 8
DIFF_HEAD_DIM = D_MODEL // (2 * DIFF_HEADS)
SB_HEADS = 16
SB_HEAD_DIM = D_MODEL // SB_HEADS
N_GROUPS = 4
EXPERTS_PER_GROUP = 8
EXPERT_SHIFT = 3
N_EXPERTS = N_GROUPS * EXPERTS_PER_GROUP
D_EXPERT = 512
NORM_EPS = 1e-6

LANES = 128
MXU_DIM = 256
VMEM_LIMIT = 56 * 1024 * 1024

ROW_TILE = 512
TQ = 768
TK_BIG = TQ
TK_MID = MXU_DIM
TK_SMALL = 128
DIFF_ROW_BLOCK = 256
SB_ROW_BLOCK = 384
LOG2E = math.log2(math.e)
SOFTPLUS_CAP = 64.0
EXPERT_TILE = 256
DMA_PRIORITIES = 2
DMA_UNROLL = 8
NEG_BIG = -1e30


def _cparams(n_axes):
    return pltpu.CompilerParams(
        dimension_semantics=("arbitrary",) * n_axes,
        vmem_limit_bytes=VMEM_LIMIT,
    )


def _rms(x, gain):
    ms = jnp.mean(x * x, axis=-1, keepdims=True)
    return x * lax.rsqrt(ms + NORM_EPS) * gain


def _norm_qkv_diff_kernel(h_ref, g_ref, w_ref, qg_ref, kg_ref, gm_ref, o_ref):
    D = D_MODEL
    xn = _rms(h_ref[...], g_ref[...]).astype(BF16)
    for part, gain_ref in ((0, qg_ref), (1, kg_ref)):
        y = jnp.dot(xn, w_ref[:, part * D:(part + 1) * D], preferred_element_type=F32)
        y2 = (y * y).astype(BF16)
        ss = jnp.concatenate(
            [jnp.dot(y2[:, c * MXU_DIM:(c + 1) * MXU_DIM], gm_ref[...],
                     preferred_element_type=F32) for c in range(D // MXU_DIM)], axis=1)
        yn = y * lax.rsqrt(ss * (1.0 / DIFF_HEAD_DIM) + NORM_EPS) * gain_ref[...]
        o_ref[:, part * D:(part + 1) * D] = yn.astype(BF16)
    v = jnp.dot(xn, w_ref[:, 2 * D:3 * D], preferred_element_type=F32)
    o_ref[:, 2 * D:3 * D] = v.astype(BF16)


def _norm_qkv_sb_kernel(h_ref, g_ref, w_ref, o_ref):
    D = D_MODEL
    xn = _rms(h_ref[...], g_ref[...]).astype(BF16)
    for part in range(3):
        y = jnp.dot(xn, w_ref[:, part * D:(part + 1) * D], preferred_element_type=F32)
        o_ref[:, part * D:(part + 1) * D] = y.astype(BF16)


def _norm_qkv(h, gain, w_bf16, qk=None):
    n, d = h.shape
    grid = (n // ROW_TILE,)
    row = pl.BlockSpec((ROW_TILE, d), lambda i: (i, 0))
    vec = pl.BlockSpec((1, d), lambda i: (0, 0))
    wspec = pl.BlockSpec((d, 3 * d), lambda i: (0, 0))
    out = pl.BlockSpec((ROW_TILE, 3 * d), lambda i: (i, 0))
    out_shape = jax.ShapeDtypeStruct((n, 3 * d), BF16)
    if qk is None:
        return pl.pallas_call(
            _norm_qkv_sb_kernel, grid=grid, in_specs=[row, vec, wspec], out_specs=out,
            out_shape=out_shape, compiler_params=_cparams(1), name="norm_qkv_sb",
        )(h, gain, w_bf16)
    qg, kg, gm = qk
    gspec = pl.BlockSpec((MXU_DIM, MXU_DIM), lambda i: (0, 0))
    return pl.pallas_call(
        _norm_qkv_diff_kernel, grid=grid, in_specs=[row, vec, wspec, vec, vec, gspec],
        out_specs=out, out_shape=out_shape, compiler_params=_cparams(1), name="norm_qkv_diff",
    )(h, gain, w_bf16, qg, kg, gm)


def _diff_attn_kernel(slopes_ref, q_ref, k_ref, v_ref, lam_ref, sg_ref, o_ref,
                      acc0, acc1, m0, m1, *, seq, lambda_init):
    hd = pl.program_id(1)
    slope = slopes_ref[hd] * LOG2E
    accs, ms = (acc0, acc1), (m0, m1)
    lp = lam_ref[...]
    lam = (jnp.exp(jnp.sum(lp[0:1] * lp[1:2], axis=-1, keepdims=True))
           - jnp.exp(jnp.sum(lp[2:3] * lp[3:4], axis=-1, keepdims=True)) + lambda_init)
    ones = jnp.ones((TK_BIG, LANES), BF16)
    lane = lax.broadcasted_iota(jnp.int32, (TQ, LANES), 1)
    dn = (((1,), (1,)), ((), ()))

    def q_body(qi, carry):
        qs = pl.multiple_of(qi * TQ, TQ)
        q = q_ref[pl.ds(qs, TQ), :]
        zero = jnp.zeros_like(q)
        qc = (jnp.where(lane < DIFF_HEAD_DIM, q, zero), jnp.where(lane >= DIFF_HEAD_DIM, q, zero))
        qpos = qs + lax.broadcasted_iota(jnp.int32, (TQ, 1), 0)
        bound = ((qpos + (CHUNK - N_META)) >> CHUNK_SHIFT) * CHUNK + N_META
        for acc, m in ((acc0, m0), (acc1, m1)):
            acc[...] = jnp.zeros_like(acc)
            m[...] = jnp.full_like(m, NEG_BIG)

        def tile(kstart, tk, masked, row0=0):
            k = k_ref[pl.ds(kstart, tk), :]
            v = v_ref[pl.ds(kstart, tk), :]
            vext = jnp.concatenate([v, ones[:tk]], axis=1)
            kpos = kstart + lax.broadcasted_iota(jnp.int32, (1, tk), 1)
            rel = (kpos - qs).astype(F32) * slope
            for r0 in range(row0, TQ, DIFF_ROW_BLOCK):
                r1 = min(r0 + DIFF_ROW_BLOCK, TQ)
                rows = pl.ds(r0, r1 - r0)
                if masked:
                    ahead = jnp.maximum(kpos - qpos[r0:r1], 0).astype(F32) * (2.0 * slope)
                    allowed = kpos < bound[r0:r1]
                for c in range(2):
                    acc, m = accs[c], ms[c]
                    s = lax.dot_general(qc[c][r0:r1], k, dn, preferred_element_type=F32) + rel
                    if masked:
                        s = jnp.where(allowed, s - ahead, -jnp.inf)
                    m_old = m[rows, :]
                    m_new = jnp.maximum(m_old, jnp.max(s, axis=1, keepdims=True))
                    alpha = jnp.exp2(m_old - m_new)
                    p = jnp.exp2(s - m_new).astype(BF16)
                    acc[rows, :] = (acc[rows, :] * alpha
                                    + jnp.dot(p, vext, preferred_element_type=F32))
                    m[rows, :] = m_new

        n_big = qi
        def big_body(j, c):
            tile(pl.multiple_of(2 * j * TK_BIG, TK_BIG), TK_BIG, False)
            tile(pl.multiple_of((2 * j + 1) * TK_BIG, TK_BIG), TK_BIG, False)
            return c
        lax.fori_loop(0, n_big // 2, big_body, 0)
        @pl.when(n_big % 2 == 1)
        def _():
            tile(pl.multiple_of((n_big - 1) * TK_BIG, TK_BIG), TK_BIG, False)
        tile(qs, TQ, True)
        @pl.when(qs + TQ < seq)
        def _():
            tile(pl.multiple_of(qs + TQ, TK_SMALL), TK_SMALL, True, TQ - CHUNK)

        a0 = acc0[...]
        a1 = acc1[...]
        o = a0[:, :LANES] / a0[:, LANES:] - lam * (a1[:, :LANES] / a1[:, LANES:])
        o = _rms(o, sg_ref[...]) * (1.0 - lambda_init)
        o_ref[pl.ds(qs, TQ), :] = o.astype(o_ref.dtype)
        return carry

    lax.fori_loop(0, seq // TQ, q_body, 0)


def _diff_attention(qkv, slopes, lam_params, sub_gain, batch, seq, lambda_init):
    H = DIFF_HEADS
    kern = functools.partial(_diff_attn_kernel, seq=seq, lambda_init=lambda_init)
    blk = lambda off: pl.BlockSpec((seq, LANES), lambda b, h, s: (b, off + h))
    return pl.pallas_call(
        kern,
        grid_spec=pltpu.PrefetchScalarGridSpec(
            num_scalar_prefetch=1, grid=(batch, H),
            in_specs=[blk(0), blk(H), blk(2 * H),
                      pl.BlockSpec((4, DIFF_HEAD_DIM), lambda b, h, s: (0, 0)),
                      pl.BlockSpec((1, LANES), lambda b, h, s: (0, 0))],
            out_specs=pl.BlockSpec((seq, LANES), lambda b, h, s: (b, h)),
            scratch_shapes=[pltpu.VMEM((TQ, 2 * LANES), F32), pltpu.VMEM((TQ, 2 * LANES), F32),
                            pltpu.VMEM((TQ, 1), F32), pltpu.VMEM((TQ, 1), F32)]),
        out_shape=jax.ShapeDtypeStruct((batch * seq, D_MODEL), BF16),
        compiler_params=_cparams(2), name="diff_attn",
    )(slopes, qkv, qkv, qkv, lam_params, sub_gain)


def _sb_attn_kernel(q_ref, k_ref, v_ref, o_ref, acc, run0, run1, *, seq):
    d = SB_HEAD_DIM
    lane = lax.broadcasted_iota(jnp.int32, (TQ, LANES), 1)
    dn = (((1,), (1,)), ((), ()))

    tk = TK_MID
    r_i = lax.broadcasted_iota(jnp.int32, (tk, tk), 0)
    c_i = lax.broadcasted_iota(jnp.int32, (tk, tk), 1)
    from_here = jnp.where(r_i >= c_i, 1.0, 0.0).astype(BF16)
    vlow = lax.broadcasted_iota(jnp.int32, (tk, LANES), 1) < d
    runs = (run0, run1)

    def q_body(qi, carry):
        qs = pl.multiple_of(qi * TQ, TQ)
        q = q_ref[pl.ds(qs, TQ), :]
        zero = jnp.zeros_like(q)
        qh = (jnp.where(lane < d, q, zero), jnp.where(lane >= d, q, zero))
        qpos = qs + lax.broadcasted_iota(jnp.int32, (TQ, 1), 0)
        acc[...] = jnp.zeros_like(acc)
        run0[...] = jnp.zeros_like(run0)
        run1[...] = jnp.zeros_like(run1)

        def tile(kstart, masked, row0=0):
            k = k_ref[pl.ds(kstart, tk), :]
            v = v_ref[pl.ds(kstart, tk), :]
            vzero = jnp.zeros_like(v)
            vh = (jnp.where(vlow, v, vzero), jnp.where(vlow, vzero, v))
            kpos = kstart + lax.broadcasted_iota(jnp.int32, (1, tk), 1)
            for r0 in range(row0, TQ, SB_ROW_BLOCK):
                r1 = min(r0 + SB_ROW_BLOCK, TQ)
                rows = pl.ds(r0, r1 - r0)
                if masked:
                    before = kpos < qpos[r0:r1]
                total = acc[rows, :]
                for hh in range(2):
                    run = runs[hh]
                    z = lax.dot_general(qh[hh][r0:r1], k, dn, preferred_element_type=F32)
                    drop = jnp.maximum(z, jnp.log2(1.0 + jnp.exp2(jnp.minimum(z, SOFTPLUS_CAP))))
                    if masked:
                        drop = jnp.where(before, drop, 0.0)
                    tail = jnp.dot(drop.astype(BF16), from_here, preferred_element_type=F32)
                    r_old = run[rows, :]
                    w = jnp.exp2((z - tail) - r_old)
                    if masked:
                        w = jnp.where(before, w, 0.0)
                    run[rows, :] = r_old + tail[:, 0:1]
                    total = total + jnp.dot(w.astype(BF16), vh[hh], preferred_element_type=F32)
                acc[rows, :] = total

        for c in reversed(range(TQ // tk)):
            tile(pl.multiple_of(qs + c * tk, tk), True, c * tk)
        n_full = qs // tk
        per = TQ // tk
        def full_body(j, c):
            for i in range(per):
                tile(pl.multiple_of((n_full - 1 - i - per * j) * tk, tk), False)
            return c
        lax.fori_loop(0, n_full // per, full_body, 0)
        o_ref[pl.ds(qs, TQ), :] = acc[...].astype(o_ref.dtype)
        return carry

    lax.fori_loop(0, seq // TQ, q_body, 0)


def _sb_attention(qkv, batch, seq):
    P = SB_HEADS // 2
    kern = functools.partial(_sb_attn_kernel, seq=seq)
    blk = lambda off: pl.BlockSpec((seq, LANES), lambda b, p: (b, off + p))
    return pl.pallas_call(
        kern, grid=(batch, P),
        in_specs=[blk(0), blk(P), blk(2 * P)],
        out_specs=pl.BlockSpec((seq, LANES), lambda b, p: (b, p)),
        scratch_shapes=[pltpu.VMEM((TQ, LANES), F32), pltpu.VMEM((TQ, 1), F32),
                        pltpu.VMEM((TQ, 1), F32)],
        out_shape=jax.ShapeDtypeStruct((batch * seq, D_MODEL), BF16),
        compiler_params=_cparams(2), name="sb_attn",
    )(qkv, qkv, qkv)


def _to_row_tiles(ref, x):
    per = x.shape[1] // LANES
    for j in range(per):
        ref[pl.ds(j, x.shape[0], stride=per), :] = x[:, j * LANES:(j + 1) * LANES]


def _from_row_tiles(ref, rows, per):
    return jnp.concatenate([ref[pl.ds(j, rows, stride=per), :] for j in range(per)], axis=1)


def _proj_res_kernel(o_ref, w_ref, h_ref, out_ref, tiles_ref):
    out = h_ref[...] + jnp.dot(o_ref[...], w_ref[...], preferred_element_type=F32)
    out_ref[...] = out
    _to_row_tiles(tiles_ref, out)


def _proj_residual(o, w_bf16, h):
    n, d = h.shape
    per = d // LANES
    row = lambda: pl.BlockSpec((ROW_TILE, d), lambda i: (i, 0))
    return pl.pallas_call(
        _proj_res_kernel, grid=(n // ROW_TILE,),
        in_specs=[row(), pl.BlockSpec((d, d), lambda i: (0, 0)), row()],
        out_specs=[row(), pl.BlockSpec((ROW_TILE * per, LANES), lambda i: (i, 0))],
        out_shape=[jax.ShapeDtypeStruct((n, d), F32),
                   jax.ShapeDtypeStruct((n * per, LANES), F32)],
        compiler_params=_cparams(1), name="proj_residual",
    )(o, w_bf16, h)


def _split3(x):
    hi = x.astype(BF16)
    return hi, (x - hi.astype(F32)).astype(BF16)


def _router_kernel(h_ref, g_ref, whi_ref, wlo_ref, b_ref, o_ref, cnt_ref):
    xn = _rms(h_ref[...], g_ref[...])
    xhi, xlo = _split3(xn)
    logits = (jnp.dot(xhi, whi_ref[...], preferred_element_type=F32)
              + jnp.dot(xhi, wlo_ref[...], preferred_element_type=F32)
              + jnp.dot(xlo, whi_ref[...], preferred_element_type=F32)) + b_ref[...]
    rows = logits.shape[0]
    lane_i = lax.broadcasted_iota(jnp.int32, (rows, LANES), 1)
    lane = lane_i.astype(F32)
    ninf = -jnp.inf
    gl = jnp.where(lane_i < N_GROUPS, logits, ninf)
    gmax = jnp.max(gl, axis=1, keepdims=True)
    gidx = jnp.min(jnp.where(gl == gmax, lane, float(LANES)), axis=1, keepdims=True)
    gsum = jnp.sum(jnp.exp(gl - gmax), axis=1, keepdims=True)
    g_p = 1.0 / gsum
    e_lane = lane_i - N_GROUPS
    lane_group = jnp.right_shift(e_lane, EXPERT_SHIFT).astype(F32)
    in_group = (e_lane >= 0) & (e_lane < N_EXPERTS) & (lane_group == gidx)
    el = jnp.where(in_group, logits, ninf)
    v1 = jnp.max(el, axis=1, keepdims=True)
    i1 = jnp.min(jnp.where(el == v1, lane, float(LANES)), axis=1, keepdims=True)
    el2 = jnp.where(lane == i1, ninf, el)
    v2 = jnp.max(el2, axis=1, keepdims=True)
    i2 = jnp.min(jnp.where(el2 == v2, lane, float(LANES)), axis=1, keepdims=True)
    t = jnp.exp(v2 - v1)
    w1 = g_p / (1.0 + t)
    w2 = g_p * t / (1.0 + t)
    @pl.when(pl.program_id(0) == 0)
    def _():
        cnt_ref[...] = jnp.zeros_like(cnt_ref)
    member = jnp.where((lane == i1) | (lane == i2), 1.0, 0.0)
    r_i = lax.broadcasted_iota(jnp.int32, (rows, rows), 0)
    c_i = lax.broadcasted_iota(jnp.int32, (rows, rows), 1)
    earlier = jnp.where(r_i > c_i, 1.0, 0.0).astype(BF16)
    before = cnt_ref[...] + jnp.dot(earlier, member.astype(BF16), preferred_element_type=F32)
    rank1 = jnp.sum(jnp.where(lane == i1, before, 0.0), axis=1, keepdims=True)
    rank2 = jnp.sum(jnp.where(lane == i2, before, 0.0), axis=1, keepdims=True)
    cnt_ref[...] = cnt_ref[...] + jnp.sum(member, axis=0, keepdims=True)
    out = jnp.where(lane_i == 0, i1 - N_GROUPS,
          jnp.where(lane_i == 1, i2 - N_GROUPS,
          jnp.where(lane_i == 2, w1,
          jnp.where(lane_i == 3, w2,
          jnp.where(lane_i == 4, rank1, jnp.where(lane_i == 5, rank2, 0.0))))))
    o_ref[...] = out


def _router(h, gain, w_hi, w_lo, bias):
    n, d = h.shape
    return pl.pallas_call(
        _router_kernel, grid=(n // ROW_TILE,),
        in_specs=[pl.BlockSpec((ROW_TILE, d), lambda i: (i, 0)),
                  pl.BlockSpec((1, d), lambda i: (0, 0)),
                  pl.BlockSpec((d, LANES), lambda i: (0, 0)),
                  pl.BlockSpec((d, LANES), lambda i: (0, 0)),
                  pl.BlockSpec((1, LANES), lambda i: (0, 0))],
        out_specs=[pl.BlockSpec((ROW_TILE, LANES), lambda i: (i, 0)),
                   pl.BlockSpec((1, LANES), lambda i: (0, 0))],
        out_shape=[jax.ShapeDtypeStruct((n, LANES), F32),
                   jax.ShapeDtypeStruct((1, LANES), F32)],
        compiler_params=_cparams(1), name="moe_router",
    )(h, gain, w_hi, w_lo, bias)


def _expert_kernel(te_ref, tok_hbm, dst_hbm, h_hbm, g_ref, wg_ref, wu_ref, wd_ref, y_hbm,
                   xbuf0, xbuf1, ybuf0, ybuf1, tok0, tok1, dst0, dst1, tsem, dsem, gsem, ssem,
                   *, n_tiles):
    t = pl.program_id(0)
    R = EXPERT_TILE
    per = g_ref.shape[1] // LANES
    xbufs, ybufs, toks, dsts = (xbuf0, xbuf1), (ybuf0, ybuf1), (tok0, tok1), (dst0, dst1)

    def tok_copy(tile, s):
        return pltpu.make_async_copy(tok_hbm.at[tile], toks[s], tsem.at[s])

    def dst_copy(tile, s):
        return pltpu.make_async_copy(dst_hbm.at[tile], dsts[s], dsem.at[s])

    def start_gather(s):
        def body(g, c):
            for u in range(DMA_UNROLL):
                r = g * DMA_UNROLL + u
                src = h_hbm.at[pl.ds(pl.multiple_of(toks[s][r], per), per)]
                dst = xbufs[s].at[pl.ds(pl.multiple_of(r * per, per), per)]
                pltpu.make_async_copy(src, dst, gsem.at[s]).start(priority=u % DMA_PRIORITIES)
            return c
        lax.fori_loop(0, R // DMA_UNROLL, body, 0)

    def start_scatter(s):
        def body(g, c):
            for u in range(DMA_UNROLL):
                r = g * DMA_UNROLL + u
                src = ybufs[s].at[pl.ds(pl.multiple_of(r * per, per), per)]
                dst = y_hbm.at[pl.ds(pl.multiple_of(dsts[s][r], per), per)]
                pltpu.make_async_copy(src, dst, ssem.at[s]).start(priority=u % DMA_PRIORITIES)
            return c
        lax.fori_loop(0, R // DMA_UNROLL, body, 0)

    def wait_gather(s):
        pltpu.make_async_copy(h_hbm.at[pl.ds(0, R * per)], xbufs[s], gsem.at[s]).wait()

    def wait_scatter(s):
        pltpu.make_async_copy(ybufs[s], y_hbm.at[pl.ds(0, R * per)], ssem.at[s]).wait()

    def step(s):
        @pl.when(t == 0)
        def _():
            tok_copy(0, 0).start()
            tok_copy(0, 0).wait()
            start_gather(0)
            for k in range(1, min(3, n_tiles)):
                tok_copy(k, k % 2).start()
            for k in range(min(2, n_tiles)):
                dst_copy(k, k).start()

        @pl.when(t + 1 < n_tiles)
        def _():
            tok_copy(t + 1, 1 - s).wait()
            start_gather(1 - s)

        @pl.when(t + 3 < n_tiles)
        def _():
            tok_copy(t + 3, 1 - s).start()

        wait_gather(s)

        @pl.when(t >= 2)
        def _():
            wait_scatter(s)

        x = _from_row_tiles(xbufs[s], R, per)
        xn = _rms(x, g_ref[...]).astype(BF16)
        a = jnp.dot(xn, wg_ref[0, 0].astype(BF16), preferred_element_type=F32)
        u = jnp.dot(xn, wu_ref[0, 0].astype(BF16), preferred_element_type=F32)
        hid = (a / (1.0 + jnp.exp(-a))) * u
        y = jnp.dot(hid.astype(BF16), wd_ref[0, 0].astype(BF16), preferred_element_type=F32)
        _to_row_tiles(ybufs[s], y)

        dst_copy(t, s).wait()
        start_scatter(s)

        @pl.when(t + 2 < n_tiles)
        def _():
            dst_copy(t + 2, s).start()

        @pl.when(t == n_tiles - 1)
        def _():
            if n_tiles > 1:
                wait_scatter(1 - s)
            wait_scatter(s)

    for s in range(2):
        pl.when(lax.rem(t, 2) == s)(functools.partial(step, s))


def _expert_mlp(tile_expert, slot_token, slot_dest, h_tiles, gain, wg, wu, wd, layer, n_out_rows):
    n_tiles = slot_token.shape[0]
    d = gain.shape[1]
    per = d // LANES
    f = wg.shape[3]
    R = EXPERT_TILE
    kern = functools.partial(_expert_kernel, n_tiles=n_tiles)
    anyspec = pl.BlockSpec(memory_space=pl.ANY)
    buf = pltpu.VMEM((R * per, LANES), F32)
    idx = pltpu.SMEM((R,), jnp.int32)
    sems = pltpu.SemaphoreType.DMA((2,))
    return pl.pallas_call(
        kern,
        grid_spec=pltpu.PrefetchScalarGridSpec(
            num_scalar_prefetch=1, grid=(n_tiles,),
            in_specs=[anyspec, anyspec, anyspec,
                      pl.BlockSpec((1, d), lambda t, te: (0, 0)),
                      pl.BlockSpec((1, 1, d, f), lambda t, te: (layer, te[t], 0, 0)),
                      pl.BlockSpec((1, 1, d, f), lambda t, te: (layer, te[t], 0, 0)),
                      pl.BlockSpec((1, 1, f, d), lambda t, te: (layer, te[t], 0, 0))],
            out_specs=anyspec,
            scratch_shapes=[buf, buf, buf, buf, idx, idx, idx, idx, sems, sems, sems, sems]),
        out_shape=jax.ShapeDtypeStruct((n_out_rows * per, LANES), F32),
        compiler_params=_cparams(1), name="moe_expert",
    )(tile_expert, slot_token, slot_dest, h_tiles, gain, wg, wu, wd)


def _combine_kernel(ya_ref, yb_ref, h_ref, r_ref, o_ref):
    rows, d = h_ref.shape
    per = d // LANES
    route = r_ref[...]
    o_ref[...] = (h_ref[...] + route[:, 2:3] * _from_row_tiles(ya_ref, rows, per)
                  + route[:, 3:4] * _from_row_tiles(yb_ref, rows, per))


def _combine(y_tiles, h, route):
    n, d = h.shape
    per = d // LANES
    steps = n // ROW_TILE
    row = pl.BlockSpec((ROW_TILE, d), lambda i: (i, 0))
    tiles = lambda off: pl.BlockSpec((ROW_TILE * per, LANES), lambda i: (off + i, 0))
    return pl.pallas_call(
        _combine_kernel, grid=(steps,),
        in_specs=[tiles(0), tiles(steps), row, pl.BlockSpec((ROW_TILE, LANES), lambda i: (i, 0))],
        out_specs=row, out_shape=jax.ShapeDtypeStruct((n, d), F32),
        compiler_params=_cparams(1), name="moe_combine",
    )(y_tiles, y_tiles, h, route)


def _moe(h, h_tiles, gain, w_group, b_group, w_router, b_router, w_gate, w_up, w_down, layer):
    n, d = h.shape
    per = d // LANES
    pad = LANES - N_GROUPS - N_EXPERTS
    w_cat = jnp.concatenate([w_group, w_router, jnp.zeros((d, pad), F32)], axis=1)
    b_cat = jnp.concatenate([b_group, b_router, jnp.zeros((pad,), F32)])[None, :]
    w_hi = w_cat.astype(BF16)
    w_lo = (w_cat - w_hi.astype(F32)).astype(BF16)
    route, lane_counts = _router(h, gain, w_hi, w_lo, b_cat)

    e_flat = route[:, 0:2].astype(jnp.int32).reshape(-1)
    rank = route[:, 4:6].astype(jnp.int32).reshape(-1)
    counts = lane_counts[0, N_GROUPS:N_GROUPS + N_EXPERTS].astype(jnp.int32)
    ptiles = (counts + EXPERT_TILE - 1) // EXPERT_TILE
    tile_end = jnp.cumsum(ptiles)
    pstart = (tile_end - ptiles) * EXPERT_TILE
    pos = pstart[e_flat] + rank
    n_tiles = (2 * n) // EXPERT_TILE + N_EXPERTS
    n_slots = n_tiles * EXPERT_TILE
    slot_pair = jnp.full((n_slots,), -1, jnp.int32).at[pos].set(
        jnp.arange(2 * n, dtype=jnp.int32))
    is_pad = slot_pair < 0
    slot_token = jnp.where(is_pad, 0, slot_pair // 2) * per
    slot_dest = jnp.where(is_pad, 2 * n - 1 + jnp.cumsum(is_pad.astype(jnp.int32)),
                          (slot_pair % 2) * n + slot_pair // 2) * per
    tile_ids = jnp.arange(n_tiles, dtype=jnp.int32)
    tile_expert = jnp.minimum(
        jnp.sum((tile_ids[:, None] >= tile_end[None, :]).astype(jnp.int32), axis=1),
        N_EXPERTS - 1).astype(jnp.int32)
    n_valid = tile_end[-1]
    last_expert = tile_expert[jnp.maximum(n_valid - 1, 0)]
    tile_expert = jnp.where(tile_ids < n_valid, tile_expert, last_expert)

    y = _expert_mlp(tile_expert, slot_token.reshape(n_tiles, EXPERT_TILE),
                    slot_dest.reshape(n_tiles, EXPERT_TILE), h_tiles, gain,
                    w_gate, w_up, w_down, layer, n_slots)
    return _combine(y, h, route)


def kernel(x, meta_tokens, norm_mix, norm_ffn, diff_w_qkv, diff_q_gain, diff_k_gain,
           diff_lambda_q1, diff_lambda_k1, diff_lambda_q2, diff_lambda_k2, diff_sub_gain,
           diff_w_o, sb_w_qkv, sb_w_o, moe_w_group, moe_b_group, moe_w_router,
           moe_b_router, moe_w_gate, moe_w_up, moe_w_down):
    B, S, D = x.shape
    L = N_META + S
    Lp = -(-L // TQ) * TQ
    depth = norm_mix.shape[0]
    assert D == D_MODEL and TQ % Q_BLOCK == 0 and TQ % TK_MID == 0 and TK_BIG == TQ
    assert (B * Lp) % ROW_TILE == 0
    assert (2 * B * Lp) % EXPERT_TILE == 0

    meta = jnp.broadcast_to(meta_tokens.astype(x.dtype)[None], (B, N_META, D))
    h = jnp.concatenate([meta, x, jnp.zeros((B, Lp - L, D), x.dtype)], axis=1).reshape(B * Lp, D)

    blockdiag = (jnp.arange(MXU_DIM)[:, None] // DIFF_HEAD_DIM
                 == jnp.arange(MXU_DIM)[None, :] // DIFF_HEAD_DIM).astype(BF16)
    slopes = jnp.exp2(-8.0 * (jnp.arange(DIFF_HEADS, dtype=F32) + 1.0) / DIFF_HEADS)

    for i in range(depth):
        j = i // N_MIXERS
        gain = norm_mix[i][None, :]
        if i % N_MIXERS == 0:
            lambda_init = 0.8 - 0.6 * math.exp(-0.3 * i)
            scale = DIFF_HEAD_DIM ** -0.5 * LOG2E
            reps = D // DIFF_HEAD_DIM
            qg = (jnp.tile(diff_q_gain[j], reps) * scale)[None, :]
            kg = jnp.tile(diff_k_gain[j], reps)[None, :]
            qkv = _norm_qkv(h, gain, diff_w_qkv[j].astype(BF16), (qg, kg, blockdiag))
            lam_params = jnp.stack([diff_lambda_q1[j], diff_lambda_k1[j],
                                    diff_lambda_q2[j], diff_lambda_k2[j]])
            o = _diff_attention(qkv, slopes, lam_params, diff_sub_gain[j][None, :], B, Lp,
                                lambda_init)
            h, h_tiles = _proj_residual(o, diff_w_o[j].astype(BF16), h)
        else:
            scale = SB_HEAD_DIM ** -0.5 * LOG2E
            colscale = jnp.concatenate([jnp.full((D,), scale, F32), jnp.ones((2 * D,), F32)])
            w = (sb_w_qkv[j] * colscale[None, :]).astype(BF16)
            qkv = _norm_qkv(h, gain, w)
            o = _sb_attention(qkv, B, Lp)
            h, h_tiles = _proj_residual(o, sb_w_o[j].astype(BF16), h)
        h = _moe(h, h_tiles, norm_ffn[i][None, :], moe_w_group[i], moe_b_group[i], moe_w_router[i],
                 moe_b_router[i], moe_w_gate, moe_w_up, moe_w_down, i)
    return h.reshape(B, Lp, D)[:, N_META:N_META + S]
```

```python
import functools
import math

import jax
import jax.numpy as jnp
from jax import lax
from jax.experimental import pallas as pl
from jax.experimental.pallas import tpu as pltpu

F32 = jnp.float32
BF16 = jnp.bfloat16

D_MODEL = 1024
CHUNK = 64
CHUNK_SHIFT = 6
N_META = 16
Q_BLOCK = 128
N_MIXERS = 2
DIFF_HEADS = 8
DIFF_HEAD_DIM = D_MODEL // (2 * DIFF_HEADS)
SB_HEADS = 16
SB_HEAD_DIM = D_MODEL // SB_HEADS
N_GROUPS = 4
EXPERTS_PER_GROUP = 8
EXPERT_SHIFT = 3
N_EXPERTS = N_GROUPS * EXPERTS_PER_GROUP
D_EXPERT = 512
NORM_EPS = 1e-6

LANES = 128
MXU_DIM = 256
VMEM_LIMIT = 56 * 1024 * 1024

ROW_TILE = 512
TQ = 768
TK_BIG = TQ
TK_MID = MXU_DIM
TK_SMALL = 128
DIFF_ROW_BLOCK = 256
SB_ROW_BLOCK = 384
LOG2E = math.log2(math.e)
SOFTPLUS_CAP = 64.0
EXPERT_TILE = 256
DMA_PRIORITIES = 2
DMA_UNROLL = 8
GATHER_PARTS = 4
NEG_BIG = -1e30


def _cparams(n_axes):
    return pltpu.CompilerParams(
        dimension_semantics=("arbitrary",) * n_axes,
        vmem_limit_bytes=VMEM_LIMIT,
    )


def _rms(x, gain):
    ms = jnp.mean(x * x, axis=-1, keepdims=True)
    return x * lax.rsqrt(ms + NORM_EPS) * gain


def _norm_qkv_diff_kernel(h_ref, g_ref, w_ref, qg_ref, kg_ref, gm_ref, o_ref):
    D = D_MODEL
    xn = _rms(h_ref[...], g_ref[...]).astype(BF16)
    for part, gain_ref in ((0, qg_ref), (1, kg_ref)):
        y = jnp.dot(xn, w_ref[:, part * D:(part + 1) * D], preferred_element_type=F32)
        y2 = (y * y).astype(BF16)
        ss = jnp.concatenate(
            [jnp.dot(y2[:, c * MXU_DIM:(c + 1) * MXU_DIM], gm_ref[...],
                     preferred_element_type=F32) for c in range(D // MXU_DIM)], axis=1)
        yn = y * lax.rsqrt(ss * (1.0 / DIFF_HEAD_DIM) + NORM_EPS) * gain_ref[...]
        o_ref[:, part * D:(part + 1) * D] = yn.astype(BF16)
    v = jnp.dot(xn, w_ref[:, 2 * D:3 * D], preferred_element_type=F32)
    o_ref[:, 2 * D:3 * D] = v.astype(BF16)


def _norm_qkv_sb_kernel(h_ref, g_ref, w_ref, o_ref):
    D = D_MODEL
    xn = _rms(h_ref[...], g_ref[...]).astype(BF16)
    for part in range(3):
        y = jnp.dot(xn, w_ref[:, part * D:(part + 1) * D], preferred_element_type=F32)
        o_ref[:, part * D:(part + 1) * D] = y.astype(BF16)


def _norm_qkv(h, gain, w_bf16, qk=None):
    n, d = h.shape
    grid = (n // ROW_TILE,)
    row = pl.BlockSpec((ROW_TILE, d), lambda i: (i, 0))
    vec = pl.BlockSpec((1, d), lambda i: (0, 0))
    wspec = pl.BlockSpec((d, 3 * d), lambda i: (0, 0))
    out = pl.BlockSpec((ROW_TILE, 3 * d), lambda i: (i, 0))
    out_shape = jax.ShapeDtypeStruct((n, 3 * d), BF16)
    if qk is None:
        return pl.pallas_call(
            _norm_qkv_sb_kernel, grid=grid, in_specs=[row, vec, wspec], out_specs=out,
            out_shape=out_shape, compiler_params=_cparams(1), name="norm_qkv_sb",
        )(h, gain, w_bf16)
    qg, kg, gm = qk
    gspec = pl.BlockSpec((MXU_DIM, MXU_DIM), lambda i: (0, 0))
    return pl.pallas_call(
        _norm_qkv_diff_kernel, grid=grid, in_specs=[row, vec, wspec, vec, vec, gspec],
        out_specs=out, out_shape=out_shape, compiler_params=_cparams(1), name="norm_qkv_diff",
    )(h, gain, w_bf16, qg, kg, gm)


def _diff_attn_kernel(slopes_ref, q_ref, k_ref, v_ref, lam_ref, sg_ref, o_ref,
                      acc0, acc1, m0, m1, *, seq, lambda_init):
    hd = pl.program_id(1)
    slope = slopes_ref[hd] * LOG2E
    accs, ms = (acc0, acc1), (m0, m1)
    lp = lam_ref[...]
    lam = (jnp.exp(jnp.sum(lp[0:1] * lp[1:2], axis=-1, keepdims=True))
           - jnp.exp(jnp.sum(lp[2:3] * lp[3:4], axis=-1, keepdims=True)) + lambda_init)
    ones = jnp.ones((TK_BIG, LANES), BF16)
    lane = lax.broadcasted_iota(jnp.int32, (TQ, LANES), 1)
    dn = (((1,), (1,)), ((), ()))

    def q_body(qi, carry):
        qs = pl.multiple_of(qi * TQ, TQ)
        q = q_ref[pl.ds(qs, TQ), :]
        zero = jnp.zeros_like(q)
        qc = (jnp.where(lane < DIFF_HEAD_DIM, q, zero), jnp.where(lane >= DIFF_HEAD_DIM, q, zero))
        qpos = qs + lax.broadcasted_iota(jnp.int32, (TQ, 1), 0)
        bound = ((qpos + (CHUNK - N_META)) >> CHUNK_SHIFT) * CHUNK + N_META
        for acc, m in ((acc0, m0), (acc1, m1)):
            acc[...] = jnp.zeros_like(acc)
            m[...] = jnp.full_like(m, NEG_BIG)

        def tile(kstart, tk, masked, row0=0):
            k = k_ref[pl.ds(kstart, tk), :]
            v = v_ref[pl.ds(kstart, tk), :]
            vext = jnp.concatenate([v, ones[:tk]], axis=1)
            kpos = kstart + lax.broadcasted_iota(jnp.int32, (1, tk), 1)
            rel = (kpos - qs).astype(F32) * slope
            for r0 in range(row0, TQ, DIFF_ROW_BLOCK):
                r1 = min(r0 + DIFF_ROW_BLOCK, TQ)
                rows = pl.ds(r0, r1 - r0)
                if masked:
                    ahead = jnp.maximum(kpos - qpos[r0:r1], 0).astype(F32) * (2.0 * slope)
                    allowed = kpos < bound[r0:r1]
                for c in range(2):
                    acc, m = accs[c], ms[c]
                    s = lax.dot_general(qc[c][r0:r1], k, dn, preferred_element_type=F32) + rel
                    if masked:
                        s = jnp.where(allowed, s - ahead, -jnp.inf)
                    m_old = m[rows, :]
                    m_new = jnp.maximum(m_old, jnp.max(s, axis=1, keepdims=True))
                    alpha = jnp.exp2(m_old - m_new)
                    p = jnp.exp2(s - m_new).astype(BF16)
                    acc[rows, :] = (acc[rows, :] * alpha
                                    + jnp.dot(p, vext, preferred_element_type=F32))
                    m[rows, :] = m_new

        n_big = qi
        def big_body(j, c):
            tile(pl.multiple_of(2 * j * TK_BIG, TK_BIG), TK_BIG, False)
            tile(pl.multiple_of((2 * j + 1) * TK_BIG, TK_BIG), TK_BIG, False)
            return c
        lax.fori_loop(0, n_big // 2, big_body, 0)
        @pl.when(n_big % 2 == 1)
        def _():
            tile(pl.multiple_of((n_big - 1) * TK_BIG, TK_BIG), TK_BIG, False)
        tile(qs, TQ, True)
        @pl.when(qs + TQ < seq)
        def _():
            tile(pl.multiple_of(qs + TQ, TK_SMALL), TK_SMALL, True, TQ - CHUNK)

        a0 = acc0[...]
        a1 = acc1[...]
        o = a0[:, :LANES] / a0[:, LANES:] - lam * (a1[:, :LANES] / a1[:, LANES:])
        o = _rms(o, sg_ref[...]) * (1.0 - lambda_init)
        o_ref[pl.ds(qs, TQ), :] = o.astype(o_ref.dtype)
        return carry

    lax.fori_loop(0, seq // TQ, q_body, 0)


def _diff_attention(qkv, slopes, lam_params, sub_gain, batch, seq, lambda_init):
    H = DIFF_HEADS
    kern = functools.partial(_diff_attn_kernel, seq=seq, lambda_init=lambda_init)
    blk = lambda off: pl.BlockSpec((seq, LANES), lambda b, h, s: (b, off + h))
    return pl.pallas_call(
        kern,
        grid_spec=pltpu.PrefetchScalarGridSpec(
            num_scalar_prefetch=1, grid=(batch, H),
            in_specs=[blk(0), blk(H), blk(2 * H),
                      pl.BlockSpec((4, DIFF_HEAD_DIM), lambda b, h, s: (0, 0)),
                      pl.BlockSpec((1, LANES), lambda b, h, s: (0, 0))],
            out_specs=pl.BlockSpec((seq, LANES), lambda b, h, s: (b, h)),
            scratch_shapes=[pltpu.VMEM((TQ, 2 * LANES), F32), pltpu.VMEM((TQ, 2 * LANES), F32),
                            pltpu.VMEM((TQ, 1), F32), pltpu.VMEM((TQ, 1), F32)]),
        out_shape=jax.ShapeDtypeStruct((batch * seq, D_MODEL), BF16),
        compiler_params=_cparams(2), name="diff_attn",
    )(slopes, qkv, qkv, qkv, lam_params, sub_gain)


def _sb_attn_kernel(q_ref, k_ref, v_ref, o_ref, acc, run0, run1, *, seq):
    d = SB_HEAD_DIM
    lane = lax.broadcasted_iota(jnp.int32, (TQ, LANES), 1)
    dn = (((1,), (1,)), ((), ()))

    tk = TK_MID
    r_i = lax.broadcasted_iota(jnp.int32, (tk, tk), 0)
    c_i = lax.broadcasted_iota(jnp.int32, (tk, tk), 1)
    from_here = jnp.where(r_i >= c_i, 1.0, 0.0).astype(BF16)
    vlow = lax.broadcasted_iota(jnp.int32, (tk, LANES), 1) < d
    runs = (run0, run1)

    def q_body(qi, carry):
        qs = pl.multiple_of(qi * TQ, TQ)
        q = q_ref[pl.ds(qs, TQ), :]
        zero = jnp.zeros_like(q)
        qh = (jnp.where(lane < d, q, zero), jnp.where(lane >= d, q, zero))
        qpos = qs + lax.broadcasted_iota(jnp.int32, (TQ, 1), 0)
        acc[...] = jnp.zeros_like(acc)
        run0[...] = jnp.zeros_like(run0)
        run1[...] = jnp.zeros_like(run1)

        def tile(kstart, masked, row0=0):
            k = k_ref[pl.ds(kstart, tk), :]
            v = v_ref[pl.ds(kstart, tk), :]
            vzero = jnp.zeros_like(v)
            vh = (jnp.where(vlow, v, vzero), jnp.where(vlow, vzero, v))
            kpos = kstart + lax.broadcasted_iota(jnp.int32, (1, tk), 1)
            for r0 in range(row0, TQ, SB_ROW_BLOCK):
                r1 = min(r0 + SB_ROW_BLOCK, TQ)
                rows = pl.ds(r0, r1 - r0)
                if masked:
                    before = kpos < qpos[r0:r1]
                total = acc[rows, :]
                for hh in range(2):
                    run = runs[hh]
                    z = lax.dot_general(qh[hh][r0:r1], k, dn, preferred_element_type=F32)
                    drop = jnp.maximum(z, jnp.log2(1.0 + jnp.exp2(jnp.minimum(z, SOFTPLUS_CAP))))
                    if masked:
                        drop = jnp.where(before, drop, 0.0)
                    tail = jnp.dot(drop.astype(BF16), from_here, preferred_element_type=F32)
                    r_old = run[rows, :]
                    w = jnp.exp2((z - tail) - r_old)
                    if masked:
                        w = jnp.where(before, w, 0.0)
                    run[rows, :] = r_old + tail[:, 0:1]
                    total = total + jnp.dot(w.astype(BF16), vh[hh], preferred_element_type=F32)
                acc[rows, :] = total

        for c in reversed(range(TQ // tk)):
            tile(pl.multiple_of(qs + c * tk, tk), True, c * tk)
        n_full = qs // tk
        per = TQ // tk
        def full_body(j, c):
            for i in range(per):
                tile(pl.multiple_of((n_full - 1 - i - per * j) * tk, tk), False)
            return c
        lax.fori_loop(0, n_full // per, full_body, 0)
        o_ref[pl.ds(qs, TQ), :] = acc[...].astype(o_ref.dtype)
        return carry

    lax.fori_loop(0, seq // TQ, q_body, 0)


def _sb_attention(qkv, batch, seq):
    P = SB_HEADS // 2
    kern = functools.partial(_sb_attn_kernel, seq=seq)
    blk = lambda off: pl.BlockSpec((seq, LANES), lambda b, p: (b, off + p))
    return pl.pallas_call(
        kern, grid=(batch, P),
        in_specs=[blk(0), blk(P), blk(2 * P)],
        out_specs=pl.BlockSpec((seq, LANES), lambda b, p: (b, p)),
        scratch_shapes=[pltpu.VMEM((TQ, LANES), F32), pltpu.VMEM((TQ, 1), F32),
                        pltpu.VMEM((TQ, 1), F32)],
        out_shape=jax.ShapeDtypeStruct((batch * seq, D_MODEL), BF16),
        compiler_params=_cparams(2), name="sb_attn",
    )(qkv, qkv, qkv)


def _to_row_tiles(ref, x):
    per = x.shape[1] // LANES
    for j in range(per):
        ref[pl.ds(j, x.shape[0], stride=per), :] = x[:, j * LANES:(j + 1) * LANES]


def _from_row_tiles(ref, rows, per):
    return jnp.concatenate([ref[pl.ds(j, rows, stride=per), :] for j in range(per)], axis=1)


def _proj_res_kernel(o_ref, w_ref, h_ref, out_ref, tiles_ref):
    out = h_ref[...] + jnp.dot(o_ref[...], w_ref[...], preferred_element_type=F32)
    out_ref[...] = out
    _to_row_tiles(tiles_ref, out)


def _proj_residual(o, w_bf16, h):
    n, d = h.shape
    per = d // LANES
    row = lambda: pl.BlockSpec((ROW_TILE, d), lambda i: (i, 0))
    return pl.pallas_call(
        _proj_res_kernel, grid=(n // ROW_TILE,),
        in_specs=[row(), pl.BlockSpec((d, d), lambda i: (0, 0)), row()],
        out_specs=[row(), pl.BlockSpec((ROW_TILE * per, LANES), lambda i: (i, 0))],
        out_shape=[jax.ShapeDtypeStruct((n, d), F32),
                   jax.ShapeDtypeStruct((n * per, LANES), F32)],
        compiler_params=_cparams(1), name="proj_residual",
    )(o, w_bf16, h)


def _split3(x):
    hi = x.astype(BF16)
    return hi, (x - hi.astype(F32)).astype(BF16)


def _router_kernel(h_ref, g_ref, whi_ref, wlo_ref, b_ref, o_ref, cnt_ref):
    xn = _rms(h_ref[...], g_ref[...])
    xhi, xlo = _split3(xn)
    logits = (jnp.dot(xhi, whi_ref[...], preferred_element_type=F32)
              + jnp.dot(xhi, wlo_ref[...], preferred_element_type=F32)
              + jnp.dot(xlo, whi_ref[...], preferred_element_type=F32)) + b_ref[...]
    rows = logits.shape[0]
    lane_i = lax.broadcasted_iota(jnp.int32, (rows, LANES), 1)
    lane = lane_i.astype(F32)
    ninf = -jnp.inf
    gl = jnp.where(lane_i < N_GROUPS, logits, ninf)
    gmax = jnp.max(gl, axis=1, keepdims=True)
    gidx = jnp.min(jnp.where(gl == gmax, lane, float(LANES)), axis=1, keepdims=True)
    gsum = jnp.sum(jnp.exp(gl - gmax), axis=1, keepdims=True)
    g_p = 1.0 / gsum
    e_lane = lane_i - N_GROUPS
    lane_group = jnp.right_shift(e_lane, EXPERT_SHIFT).astype(F32)
    in_group = (e_lane >= 0) & (e_lane < N_EXPERTS) & (lane_group == gidx)
    el = jnp.where(in_group, logits, ninf)
    v1 = jnp.max(el, axis=1, keepdims=True)
    i1 = jnp.min(jnp.where(el == v1, lane, float(LANES)), axis=1, keepdims=True)
    el2 = jnp.where(lane == i1, ninf, el)
    v2 = jnp.max(el2, axis=1, keepdims=True)
    i2 = jnp.min(jnp.where(el2 == v2, lane, float(LANES)), axis=1, keepdims=True)
    t = jnp.exp(v2 - v1)
    w1 = g_p / (1.0 + t)
    w2 = g_p * t / (1.0 + t)
    @pl.when(pl.program_id(0) == 0)
    def _():
        cnt_ref[...] = jnp.zeros_like(cnt_ref)
    member = jnp.where((lane == i1) | (lane == i2), 1.0, 0.0)
    r_i = lax.broadcasted_iota(jnp.int32, (rows, rows), 0)
    c_i = lax.broadcasted_iota(jnp.int32, (rows, rows), 1)
    earlier = jnp.where(r_i > c_i, 1.0, 0.0).astype(BF16)
    before = cnt_ref[...] + jnp.dot(earlier, member.astype(BF16), preferred_element_type=F32)
    rank1 = jnp.sum(jnp.where(lane == i1, before, 0.0), axis=1, keepdims=True)
    rank2 = jnp.sum(jnp.where(lane == i2, before, 0.0), axis=1, keepdims=True)
    cnt_ref[...] = cnt_ref[...] + jnp.sum(member, axis=0, keepdims=True)
    out = jnp.where(lane_i == 0, i1 - N_GROUPS,
          jnp.where(lane_i == 1, i2 - N_GROUPS,
          jnp.where(lane_i == 2, w1,
          jnp.where(lane_i == 3, w2,
          jnp.where(lane_i == 4, rank1, jnp.where(lane_i == 5, rank2, 0.0))))))
    o_ref[...] = out


def _router(h, gain, w_hi, w_lo, bias):
    n, d = h.shape
    return pl.pallas_call(
        _router_kernel, grid=(n // ROW_TILE,),
        in_specs=[pl.BlockSpec((ROW_TILE, d), lambda i: (i, 0)),
                  pl.BlockSpec((1, d), lambda i: (0, 0)),
                  pl.BlockSpec((d, LANES), lambda i: (0, 0)),
                  pl.BlockSpec((d, LANES), lambda i: (0, 0)),
                  pl.BlockSpec((1, LANES), lambda i: (0, 0))],
        out_specs=[pl.BlockSpec((ROW_TILE, LANES), lambda i: (i, 0)),
                   pl.BlockSpec((1, LANES), lambda i: (0, 0))],
        out_shape=[jax.ShapeDtypeStruct((n, LANES), F32),
                   jax.ShapeDtypeStruct((1, LANES), F32)],
        compiler_params=_cparams(1), name="moe_router",
    )(h, gain, w_hi, w_lo, bias)


def _expert_kernel(te_ref, tok_hbm, dst_hbm, h_hbm, g_ref, wg_ref, wu_ref, wd_ref, y_hbm,
                   xbuf0, xbuf1, ybuf0, ybuf1, tok0, tok1, dst0, dst1, tsem, dsem, gsem, ssem,
                   *, n_tiles):
    t = pl.program_id(0)
    R = EXPERT_TILE
    per = g_ref.shape[1] // LANES
    xbufs, ybufs, toks, dsts = (xbuf0, xbuf1), (ybuf0, ybuf1), (tok0, tok1), (dst0, dst1)

    def tok_copy(tile, s):
        return pltpu.make_async_copy(tok_hbm.at[tile], toks[s], tsem.at[s])

    def dst_copy(tile, s):
        return pltpu.make_async_copy(dst_hbm.at[tile], dsts[s], dsem.at[s])

    def start_gather(s, part=0, parts=1):
        trips = R // DMA_UNROLL // parts
        def body(g, c):
            for u in range(DMA_UNROLL):
                r = g * DMA_UNROLL + u
                src = h_hbm.at[pl.ds(pl.multiple_of(toks[s][r], per), per)]
                dst = xbufs[s].at[pl.ds(pl.multiple_of(r * per, per), per)]
                pltpu.make_async_copy(src, dst, gsem.at[s]).start(priority=u % DMA_PRIORITIES)
            return c
        lax.fori_loop(part * trips, (part + 1) * trips, body, 0)

    def start_scatter(s):
        def body(g, c):
            for u in range(DMA_UNROLL):
                r = g * DMA_UNROLL + u
                src = ybufs[s].at[pl.ds(pl.multiple_of(r * per, per), per)]
                dst = y_hbm.at[pl.ds(pl.multiple_of(dsts[s][r], per), per)]
                pltpu.make_async_copy(src, dst, ssem.at[s]).start(priority=u % DMA_PRIORITIES)
            return c
        lax.fori_loop(0, R // DMA_UNROLL, body, 0)

    def wait_gather(s):
        pltpu.make_async_copy(h_hbm.at[pl.ds(0, R * per)], xbufs[s], gsem.at[s]).wait()

    def wait_scatter(s):
        pltpu.make_async_copy(ybufs[s], y_hbm.at[pl.ds(0, R * per)], ssem.at[s]).wait()

    def step(s):
        @pl.when(t == 0)
        def _():
            tok_copy(0, 0).start()
            tok_copy(0, 0).wait()
            start_gather(0)
            for k in range(1, min(3, n_tiles)):
                tok_copy(k, k % 2).start()
            for k in range(min(2, n_tiles)):
                dst_copy(k, k).start()

        def gather_next(part):
            @pl.when(t + 1 < n_tiles)
            def _():
                if part == 0:
                    tok_copy(t + 1, 1 - s).wait()
                start_gather(1 - s, part, GATHER_PARTS)

        wait_gather(s)

        @pl.when(t >= 2)
        def _():
            wait_scatter(s)

        x = _from_row_tiles(xbufs[s], R, per)
        xn = _rms(x, g_ref[...]).astype(BF16)
        gather_next(0)
        a = jnp.dot(xn, wg_ref[0, 0].astype(BF16), preferred_element_type=F32)
        gather_next(1)
        u = jnp.dot(xn, wu_ref[0, 0].astype(BF16), preferred_element_type=F32)
        hid = (a / (1.0 + jnp.exp(-a))) * u
        gather_next(2)
        y = jnp.dot(hid.astype(BF16), wd_ref[0, 0].astype(BF16), preferred_element_type=F32)
        gather_next(3)
        _to_row_tiles(ybufs[s], y)

        @pl.when(t + 3 < n_tiles)
        def _():
            tok_copy(t + 3, 1 - s).start()

        dst_copy(t, s).wait()
        start_scatter(s)

        @pl.when(t + 2 < n_tiles)
        def _():
            dst_copy(t + 2, s).start()

        @pl.when(t == n_tiles - 1)
        def _():
            if n_tiles > 1:
                wait_scatter(1 - s)
            wait_scatter(s)

    for s in range(2):
        pl.when(lax.rem(t, 2) == s)(functools.partial(step, s))


def _expert_mlp(tile_expert, slot_token, slot_dest, h_tiles, gain, wg, wu, wd, layer, n_out_rows):
    n_tiles = slot_token.shape[0]
    d = gain.shape[1]
    per = d // LANES
    f = wg.shape[3]
    R = EXPERT_TILE
    assert R % (DMA_UNROLL * GATHER_PARTS) == 0
    kern = functools.partial(_expert_kernel, n_tiles=n_tiles)
    anyspec = pl.BlockSpec(memory_space=pl.ANY)
    buf = pltpu.VMEM((R * per, LANES), F32)
    idx = pltpu.SMEM((R,), jnp.int32)
    sems = pltpu.SemaphoreType.DMA((2,))
    return pl.pallas_call(
        kern,
        grid_spec=pltpu.PrefetchScalarGridSpec(
            num_scalar_prefetch=1, grid=(n_tiles,),
            in_specs=[anyspec, anyspec, anyspec,
                      pl.BlockSpec((1, d), lambda t, te: (0, 0)),
                      pl.BlockSpec((1, 1, d, f), lambda t, te: (layer, te[t], 0, 0)),
                      pl.BlockSpec((1, 1, d, f), lambda t, te: (layer, te[t], 0, 0)),
                      pl.BlockSpec((1, 1, f, d), lambda t, te: (layer, te[t], 0, 0))],
            out_specs=anyspec,
            scratch_shapes=[buf, buf, buf, buf, idx, idx, idx, idx, sems, sems, sems, sems]),
        out_shape=jax.ShapeDtypeStruct((n_out_rows * per, LANES), F32),
        compiler_params=_cparams(1), name="moe_expert",
    )(tile_expert, slot_token, slot_dest, h_tiles, gain, wg, wu, wd)


def _combine_kernel(ya_ref, yb_ref, h_ref, r_ref, o_ref):
    rows, d = h_ref.shape
    per = d // LANES
    route = r_ref[...]
    o_ref[...] = (h_ref[...] + route[:, 2:3] * _from_row_tiles(ya_ref, rows, per)
                  + route[:, 3:4] * _from_row_tiles(yb_ref, rows, per))


def _combine(y_tiles, h, route):
    n, d = h.shape
    per = d // LANES
    steps = n // ROW_TILE
    row = pl.BlockSpec((ROW_TILE, d), lambda i: (i, 0))
    tiles = lambda off: pl.BlockSpec((ROW_TILE * per, LANES), lambda i: (off + i, 0))
    return pl.pallas_call(
        _combine_kernel, grid=(steps,),
        in_specs=[tiles(0), tiles(steps), row, pl.BlockSpec((ROW_TILE, LANES), lambda i: (i, 0))],
        out_specs=row, out_shape=jax.ShapeDtypeStruct((n, d), F32),
        compiler_params=_cparams(1), name="moe_combine",
    )(y_tiles, y_tiles, h, route)


def _moe(h, h_tiles, gain, w_group, b_group, w_router, b_router, w_gate, w_up, w_down, layer):
    n, d = h.shape
    per = d // LANES
    pad = LANES - N_GROUPS - N_EXPERTS
    w_cat = jnp.concatenate([w_group, w_router, jnp.zeros((d, pad), F32)], axis=1)
    b_cat = jnp.concatenate([b_group, b_router, jnp.zeros((pad,), F32)])[None, :]
    w_hi = w_cat.astype(BF16)
    w_lo = (w_cat - w_hi.astype(F32)).astype(BF16)
    route, lane_counts = _router(h, gain, w_hi, w_lo, b_cat)

    e_flat = route[:, 0:2].astype(jnp.int32).reshape(-1)
    rank = route[:, 4:6].astype(jnp.int32).reshape(-1)
    counts = lane_counts[0, N_GROUPS:N_GROUPS + N_EXPERTS].astype(jnp.int32)
    ptiles = (counts + EXPERT_TILE - 1) // EXPERT_TILE
    tile_end = jnp.cumsum(ptiles)
    pstart = (tile_end - ptiles) * EXPERT_TILE
    pos = pstart[e_flat] + rank
    n_tiles = (2 * n) // EXPERT_TILE + N_EXPERTS
    n_slots = n_tiles * EXPERT_TILE
    slot_pair = jnp.full((n_slots,), -1, jnp.int32).at[pos].set(
        jnp.arange(2 * n, dtype=jnp.int32))
    is_pad = slot_pair < 0
    slot_token = jnp.where(is_pad, 0, slot_pair // 2) * per
    slot_dest = jnp.where(is_pad, 2 * n - 1 + jnp.cumsum(is_pad.astype(jnp.int32)),
                          (slot_pair % 2) * n + slot_pair // 2) * per
    tile_ids = jnp.arange(n_tiles, dtype=jnp.int32)
    tile_expert = jnp.minimum(
        jnp.sum((tile_ids[:, None] >= tile_end[None, :]).astype(jnp.int32), axis=1),
        N_EXPERTS - 1).astype(jnp.int32)
    n_valid = tile_end[-1]
    last_expert = tile_expert[jnp.maximum(n_valid - 1, 0)]
    tile_expert = jnp.where(tile_ids < n_valid, tile_expert, last_expert)

    y = _expert_mlp(tile_expert, slot_token.reshape(n_tiles, EXPERT_TILE),
                    slot_dest.reshape(n_tiles, EXPERT_TILE), h_tiles, gain,
                    w_gate, w_up, w_down, layer, n_slots)
    return _combine(y, h, route)


def kernel(x, meta_tokens, norm_mix, norm_ffn, diff_w_qkv, diff_q_gain, diff_k_gain,
           diff_lambda_q1, diff_lambda_k1, diff_lambda_q2, diff_lambda_k2, diff_sub_gain,
           diff_w_o, sb_w_qkv, sb_w_o, moe_w_group, moe_b_group, moe_w_router,
           moe_b_router, moe_w_gate, moe_w_up, moe_w_down):
    B, S, D = x.shape
    L = N_META + S
    Lp = -(-L // TQ) * TQ
    depth = norm_mix.shape[0]
    assert D == D_MODEL and TQ % Q_BLOCK == 0 and TQ % TK_MID == 0 and TK_BIG == TQ
    assert (B * Lp) % ROW_TILE == 0
    assert (2 * B * Lp) % EXPERT_TILE == 0

    meta = jnp.broadcast_to(meta_tokens.astype(x.dtype)[None], (B, N_META, D))
    h = jnp.concatenate([meta, x, jnp.zeros((B, Lp - L, D), x.dtype)], axis=1).reshape(B * Lp, D)

    blockdiag = (jnp.arange(MXU_DIM)[:, None] // DIFF_HEAD_DIM
                 == jnp.arange(MXU_DIM)[None, :] // DIFF_HEAD_DIM).astype(BF16)
    slopes = jnp.exp2(-8.0 * (jnp.arange(DIFF_HEADS, dtype=F32) + 1.0) / DIFF_HEADS)

    for i in range(depth):
        j = i // N_MIXERS
        gain = norm_mix[i][None, :]
        if i % N_MIXERS == 0:
            lambda_init = 0.8 - 0.6 * math.exp(-0.3 * i)
            scale = DIFF_HEAD_DIM ** -0.5 * LOG2E
            reps = D // DIFF_HEAD_DIM
            qg = (jnp.tile(diff_q_gain[j], reps) * scale)[None, :]
            kg = jnp.tile(diff_k_gain[j], reps)[None, :]
            qkv = _norm_qkv(h, gain, diff_w_qkv[j].astype(BF16), (qg, kg, blockdiag))
            lam_params = jnp.stack([diff_lambda_q1[j], diff_lambda_k1[j],
                                    diff_lambda_q2[j], diff_lambda_k2[j]])
            o = _diff_attention(qkv, slopes, lam_params, diff_sub_gain[j][None, :], B, Lp,
                                lambda_init)
            h, h_tiles = _proj_residual(o, diff_w_o[j].astype(BF16), h)
        else:
            scale = SB_HEAD_DIM ** -0.5 * LOG2E
            colscale = jnp.concatenate([jnp.full((D,), scale, F32), jnp.ones((2 * D,), F32)])
            w = (sb_w_qkv[j] * colscale[None, :]).astype(BF16)
            qkv = _norm_qkv(h, gain, w)
            o = _sb_attention(qkv, B, Lp)
            h, h_tiles = _proj_residual(o, sb_w_o[j].astype(BF16), h)
        h = _moe(h, h_tiles, norm_ffn[i][None, :], moe_w_group[i], moe_b_group[i], moe_w_router[i],
                 moe_b_router[i], moe_w_gate, moe_w_up, moe_w_down, i)
    return h.reshape(B, Lp, D)[:, N_META:N_META + S]
```

```python
import functools
import math

import jax
import jax.numpy as jnp
from jax import lax
from jax.experimental import pallas as pl
from jax.experimental.pallas import tpu as pltpu

F32 = jnp.float32
BF16 = jnp.bfloat16

D_MODEL = 1024
CHUNK = 64
CHUNK_SHIFT = 6
N_META = 16
Q_BLOCK = 128
N_MIXERS = 2
DIFF_HEADS = 8
DIFF_HEAD_DIM = D_MODEL // (2 * DIFF_HEADS)
SB_HEADS = 16
SB_HEAD_DIM = D_MODEL // SB_HEADS
N_GROUPS = 4
EXPERTS_PER_GROUP = 8
EXPERT_SHIFT = 3
N_EXPERTS = N_GROUPS * EXPERTS_PER_GROUP
D_EXPERT = 512
NORM_EPS = 1e-6

LANES = 128
MXU_DIM = 256
VMEM_LIMIT = 56 * 1024 * 1024

ROW_TILE = 512
TQ = 768
TK_BIG = TQ
TK_MID = MXU_DIM
TK_SMALL = 128
DIFF_ROW_BLOCK = 256
SB_ROW_BLOCK = 384
LOG2E = math.log2(math.e)
SOFTPLUS_CAP = 64.0
EXPERT_TILE = 128
DMA_PRIORITIES = 2
DMA_UNROLL = 8
NEG_BIG = -1e30


def _cparams(n_axes):
    return pltpu.CompilerParams(
        dimension_semantics=("arbitrary",) * n_axes,
        vmem_limit_bytes=VMEM_LIMIT,
    )


def _rms(x, gain):
    ms = jnp.mean(x * x, axis=-1, keepdims=True)
    return x * lax.rsqrt(ms + NORM_EPS) * gain


def _norm_qkv_diff_kernel(h_ref, g_ref, w_ref, qg_ref, kg_ref, gm_ref, o_ref):
    D = D_MODEL
    xn = _rms(h_ref[...], g_ref[...]).astype(BF16)
    for part, gain_ref in ((0, qg_ref), (1, kg_ref)):
        y = jnp.dot(xn, w_ref[:, part * D:(part + 1) * D], preferred_element_type=F32)
        y2 = (y * y).astype(BF16)
        ss = jnp.concatenate(
            [jnp.dot(y2[:, c * MXU_DIM:(c + 1) * MXU_DIM], gm_ref[...],
                     preferred_element_type=F32) for c in range(D // MXU_DIM)], axis=1)
        yn = y * lax.rsqrt(ss * (1.0 / DIFF_HEAD_DIM) + NORM_EPS) * gain_ref[...]
        o_ref[:, part * D:(part + 1) * D] = yn.astype(BF16)
    v = jnp.dot(xn, w_ref[:, 2 * D:3 * D], preferred_element_type=F32)
    o_ref[:, 2 * D:3 * D] = v.astype(BF16)


def _norm_qkv_sb_kernel(h_ref, g_ref, w_ref, o_ref):
    D = D_MODEL
    xn = _rms(h_ref[...], g_ref[...]).astype(BF16)
    for part in range(3):
        y = jnp.dot(xn, w_ref[:, part * D:(part + 1) * D], preferred_element_type=F32)
        o_ref[:, part * D:(part + 1) * D] = y.astype(BF16)


def _norm_qkv(h, gain, w_bf16, qk=None):
    n, d = h.shape
    grid = (n // ROW_TILE,)
    row = pl.BlockSpec((ROW_TILE, d), lambda i: (i, 0))
    vec = pl.BlockSpec((1, d), lambda i: (0, 0))
    wspec = pl.BlockSpec((d, 3 * d), lambda i: (0, 0))
    out = pl.BlockSpec((ROW_TILE, 3 * d), lambda i: (i, 0))
    out_shape = jax.ShapeDtypeStruct((n, 3 * d), BF16)
    if qk is None:
        return pl.pallas_call(
            _norm_qkv_sb_kernel, grid=grid, in_specs=[row, vec, wspec], out_specs=out,
            out_shape=out_shape, compiler_params=_cparams(1), name="norm_qkv_sb",
        )(h, gain, w_bf16)
    qg, kg, gm = qk
    gspec = pl.BlockSpec((MXU_DIM, MXU_DIM), lambda i: (0, 0))
    return pl.pallas_call(
        _norm_qkv_diff_kernel, grid=grid, in_specs=[row, vec, wspec, vec, vec, gspec],
        out_specs=out, out_shape=out_shape, compiler_params=_cparams(1), name="norm_qkv_diff",
    )(h, gain, w_bf16, qg, kg, gm)


def _diff_attn_kernel(slopes_ref, q_ref, k_ref, v_ref, lam_ref, sg_ref, o_ref,
                      acc0, acc1, m0, m1, *, seq, lambda_init):
    hd = pl.program_id(1)
    slope = slopes_ref[hd] * LOG2E
    accs, ms = (acc0, acc1), (m0, m1)
    lp = lam_ref[...]
    lam = (jnp.exp(jnp.sum(lp[0:1] * lp[1:2], axis=-1, keepdims=True))
           - jnp.exp(jnp.sum(lp[2:3] * lp[3:4], axis=-1, keepdims=True)) + lambda_init)
    ones = jnp.ones((TK_BIG, LANES), BF16)
    lane = lax.broadcasted_iota(jnp.int32, (TQ, LANES), 1)
    dn = (((1,), (1,)), ((), ()))

    def q_body(qi, carry):
        qs = pl.multiple_of(qi * TQ, TQ)
        q = q_ref[pl.ds(qs, TQ), :]
        zero = jnp.zeros_like(q)
        qc = (jnp.where(lane < DIFF_HEAD_DIM, q, zero), jnp.where(lane >= DIFF_HEAD_DIM, q, zero))
        qpos = qs + lax.broadcasted_iota(jnp.int32, (TQ, 1), 0)
        bound = ((qpos + (CHUNK - N_META)) >> CHUNK_SHIFT) * CHUNK + N_META
        for acc, m in ((acc0, m0), (acc1, m1)):
            acc[...] = jnp.zeros_like(acc)
            m[...] = jnp.full_like(m, NEG_BIG)

        def tile(kstart, tk, masked, row0=0):
            k = k_ref[pl.ds(kstart, tk), :]
            v = v_ref[pl.ds(kstart, tk), :]
            vext = jnp.concatenate([v, ones[:tk]], axis=1)
            kpos = kstart + lax.broadcasted_iota(jnp.int32, (1, tk), 1)
            rel = (kpos - qs).astype(F32) * slope
            for r0 in range(row0, TQ, DIFF_ROW_BLOCK):
                r1 = min(r0 + DIFF_ROW_BLOCK, TQ)
                rows = pl.ds(r0, r1 - r0)
                if masked:
                    ahead = jnp.maximum(kpos - qpos[r0:r1], 0).astype(F32) * (2.0 * slope)
                    allowed = kpos < bound[r0:r1]
                for c in range(2):
                    acc, m = accs[c], ms[c]
                    s = lax.dot_general(qc[c][r0:r1], k, dn, preferred_element_type=F32) + rel
                    if masked:
                        s = jnp.where(allowed, s - ahead, -jnp.inf)
                    m_old = m[rows, :]
                    m_new = jnp.maximum(m_old, jnp.max(s, axis=1, keepdims=True))
                    alpha = jnp.exp2(m_old - m_new)
                    p = jnp.exp2(s - m_new).astype(BF16)
                    acc[rows, :] = (acc[rows, :] * alpha
                                    + jnp.dot(p, vext, preferred_element_type=F32))
                    m[rows, :] = m_new

        n_big = qi
        def big_body(j, c):
            tile(pl.multiple_of(2 * j * TK_BIG, TK_BIG), TK_BIG, False)
            tile(pl.multiple_of((2 * j + 1) * TK_BIG, TK_BIG), TK_BIG, False)
            return c
        lax.fori_loop(0, n_big // 2, big_body, 0)
        @pl.when(n_big % 2 == 1)
        def _():
            tile(pl.multiple_of((n_big - 1) * TK_BIG, TK_BIG), TK_BIG, False)
        tile(qs, TQ, True)
        @pl.when(qs + TQ < seq)
        def _():
            tile(pl.multiple_of(qs + TQ, TK_SMALL), TK_SMALL, True, TQ - CHUNK)

        a0 = acc0[...]
        a1 = acc1[...]
        o = a0[:, :LANES] / a0[:, LANES:] - lam * (a1[:, :LANES] / a1[:, LANES:])
        o = _rms(o, sg_ref[...]) * (1.0 - lambda_init)
        o_ref[pl.ds(qs, TQ), :] = o.astype(o_ref.dtype)
        return carry

    lax.fori_loop(0, seq // TQ, q_body, 0)


def _diff_attention(qkv, slopes, lam_params, sub_gain, batch, seq, lambda_init):
    H = DIFF_HEADS
    kern = functools.partial(_diff_attn_kernel, seq=seq, lambda_init=lambda_init)
    blk = lambda off: pl.BlockSpec((seq, LANES), lambda b, h, s: (b, off + h))
    return pl.pallas_call(
        kern,
        grid_spec=pltpu.PrefetchScalarGridSpec(
            num_scalar_prefetch=1, grid=(batch, H),
            in_specs=[blk(0), blk(H), blk(2 * H),
                      pl.BlockSpec((4, DIFF_HEAD_DIM), lambda b, h, s: (0, 0)),
                      pl.BlockSpec((1, LANES), lambda b, h, s: (0, 0))],
            out_specs=pl.BlockSpec((seq, LANES), lambda b, h, s: (b, h)),
            scratch_shapes=[pltpu.VMEM((TQ, 2 * LANES), F32), pltpu.VMEM((TQ, 2 * LANES), F32),
                            pltpu.VMEM((TQ, 1), F32), pltpu.VMEM((TQ, 1), F32)]),
        out_shape=jax.ShapeDtypeStruct((batch * seq, D_MODEL), BF16),
        compiler_params=_cparams(2), name="diff_attn",
    )(slopes, qkv, qkv, qkv, lam_params, sub_gain)


def _sb_attn_kernel(q_ref, k_ref, v_ref, o_ref, acc, run0, run1, *, seq):
    d = SB_HEAD_DIM
    lane = lax.broadcasted_iota(jnp.int32, (TQ, LANES), 1)
    dn = (((1,), (1,)), ((), ()))

    tk = TK_MID
    r_i = lax.broadcasted_iota(jnp.int32, (tk, tk), 0)
    c_i = lax.broadcasted_iota(jnp.int32, (tk, tk), 1)
    from_here = jnp.where(r_i >= c_i, 1.0, 0.0).astype(BF16)
    vlow = lax.broadcasted_iota(jnp.int32, (tk, LANES), 1) < d
    runs = (run0, run1)

    def q_body(qi, carry):
        qs = pl.multiple_of(qi * TQ, TQ)
        q = q_ref[pl.ds(qs, TQ), :]
        zero = jnp.zeros_like(q)
        qh = (jnp.where(lane < d, q, zero), jnp.where(lane >= d, q, zero))
        qpos = qs + lax.broadcasted_iota(jnp.int32, (TQ, 1), 0)
        acc[...] = jnp.zeros_like(acc)
        run0[...] = jnp.zeros_like(run0)
        run1[...] = jnp.zeros_like(run1)

        def tile(kstart, masked, row0=0):
            k = k_ref[pl.ds(kstart, tk), :]
            v = v_ref[pl.ds(kstart, tk), :]
            vzero = jnp.zeros_like(v)
            vh = (jnp.where(vlow, v, vzero), jnp.where(vlow, vzero, v))
            kpos = kstart + lax.broadcasted_iota(jnp.int32, (1, tk), 1)
            for r0 in range(row0, TQ, SB_ROW_BLOCK):
                r1 = min(r0 + SB_ROW_BLOCK, TQ)
                rows = pl.ds(r0, r1 - r0)
                if masked:
                    before = kpos < qpos[r0:r1]
                total = acc[rows, :]
                for hh in range(2):
                    run = runs[hh]
                    z = lax.dot_general(qh[hh][r0:r1], k, dn, preferred_element_type=F32)
                    drop = jnp.maximum(z, jnp.log2(1.0 + jnp.exp2(jnp.minimum(z, SOFTPLUS_CAP))))
                    if masked:
                        drop = jnp.where(before, drop, 0.0)
                    tail = jnp.dot(drop.astype(BF16), from_here, preferred_element_type=F32)
                    r_old = run[rows, :]
                    w = jnp.exp2((z - tail) - r_old)
                    if masked:
                        w = jnp.where(before, w, 0.0)
                    run[rows, :] = r_old + tail[:, 0:1]
                    total = total + jnp.dot(w.astype(BF16), vh[hh], preferred_element_type=F32)
                acc[rows, :] = total

        for c in reversed(range(TQ // tk)):
            tile(pl.multiple_of(qs + c * tk, tk), True, c * tk)
        n_full = qs // tk
        per = TQ // tk
        def full_body(j, c):
            for i in range(per):
                tile(pl.multiple_of((n_full - 1 - i - per * j) * tk, tk), False)
            return c
        lax.fori_loop(0, n_full // per, full_body, 0)
        o_ref[pl.ds(qs, TQ), :] = acc[...].astype(o_ref.dtype)
        return carry

    lax.fori_loop(0, seq // TQ, q_body, 0)


def _sb_attention(qkv, batch, seq):
    P = SB_HEADS // 2
    kern = functools.partial(_sb_attn_kernel, seq=seq)
    blk = lambda off: pl.BlockSpec((seq, LANES), lambda b, p: (b, off + p))
    return pl.pallas_call(
        kern, grid=(batch, P),
        in_specs=[blk(0), blk(P), blk(2 * P)],
        out_specs=pl.BlockSpec((seq, LANES), lambda b, p: (b, p)),
        scratch_shapes=[pltpu.VMEM((TQ, LANES), F32), pltpu.VMEM((TQ, 1), F32),
                        pltpu.VMEM((TQ, 1), F32)],
        out_shape=jax.ShapeDtypeStruct((batch * seq, D_MODEL), BF16),
        compiler_params=_cparams(2), name="sb_attn",
    )(qkv, qkv, qkv)


def _to_row_tiles(ref, x):
    per = x.shape[1] // LANES
    for j in range(per):
        ref[pl.ds(j, x.shape[0], stride=per), :] = x[:, j * LANES:(j + 1) * LANES]


def _from_row_tiles(ref, rows, per):
    return jnp.concatenate([ref[pl.ds(j, rows, stride=per), :] for j in range(per)], axis=1)


def _proj_res_kernel(o_ref, w_ref, h_ref, out_ref, tiles_ref):
    out = h_ref[...] + jnp.dot(o_ref[...], w_ref[...], preferred_element_type=F32)
    out_ref[...] = out
    _to_row_tiles(tiles_ref, out)


def _proj_residual(o, w_bf16, h):
    n, d = h.shape
    per = d // LANES
    row = lambda: pl.BlockSpec((ROW_TILE, d), lambda i: (i, 0))
    return pl.pallas_call(
        _proj_res_kernel, grid=(n // ROW_TILE,),
        in_specs=[row(), pl.BlockSpec((d, d), lambda i: (0, 0)), row()],
        out_specs=[row(), pl.BlockSpec((ROW_TILE * per, LANES), lambda i: (i, 0))],
        out_shape=[jax.ShapeDtypeStruct((n, d), F32),
                   jax.ShapeDtypeStruct((n * per, LANES), F32)],
        compiler_params=_cparams(1), name="proj_residual",
    )(o, w_bf16, h)


def _split3(x):
    hi = x.astype(BF16)
    return hi, (x - hi.astype(F32)).astype(BF16)


def _router_kernel(h_ref, g_ref, whi_ref, wlo_ref, b_ref, o_ref, cnt_ref):
    xn = _rms(h_ref[...], g_ref[...])
    xhi, xlo = _split3(xn)
    logits = (jnp.dot(xhi, whi_ref[...], preferred_element_type=F32)
              + jnp.dot(xhi, wlo_ref[...], preferred_element_type=F32)
              + jnp.dot(xlo, whi_ref[...], preferred_element_type=F32)) + b_ref[...]
    rows = logits.shape[0]
    lane_i = lax.broadcasted_iota(jnp.int32, (rows, LANES), 1)
    lane = lane_i.astype(F32)
    ninf = -jnp.inf
    gl = jnp.where(lane_i < N_GROUPS, logits, ninf)
    gmax = jnp.max(gl, axis=1, keepdims=True)
    gidx = jnp.min(jnp.where(gl == gmax, lane, float(LANES)), axis=1, keepdims=True)
    gsum = jnp.sum(jnp.exp(gl - gmax), axis=1, keepdims=True)
    g_p = 1.0 / gsum
    e_lane = lane_i - N_GROUPS
    lane_group = jnp.right_shift(e_lane, EXPERT_SHIFT).astype(F32)
    in_group = (e_lane >= 0) & (e_lane < N_EXPERTS) & (lane_group == gidx)
    el = jnp.where(in_group, logits, ninf)
    v1 = jnp.max(el, axis=1, keepdims=True)
    i1 = jnp.min(jnp.where(el == v1, lane, float(LANES)), axis=1, keepdims=True)
    el2 = jnp.where(lane == i1, ninf, el)
    v2 = jnp.max(el2, axis=1, keepdims=True)
    i2 = jnp.min(jnp.where(el2 == v2, lane, float(LANES)), axis=1, keepdims=True)
    t = jnp.exp(v2 - v1)
    w1 = g_p / (1.0 + t)
    w2 = g_p * t / (1.0 + t)
    @pl.when(pl.program_id(0) == 0)
    def _():
        cnt_ref[...] = jnp.zeros_like(cnt_ref)
    member = jnp.where((lane == i1) | (lane == i2), 1.0, 0.0)
    r_i = lax.broadcasted_iota(jnp.int32, (rows, rows), 0)
    c_i = lax.broadcasted_iota(jnp.int32, (rows, rows), 1)
    earlier = jnp.where(r_i > c_i, 1.0, 0.0).astype(BF16)
    before = cnt_ref[...] + jnp.dot(earlier, member.astype(BF16), preferred_element_type=F32)
    rank1 = jnp.sum(jnp.where(lane == i1, before, 0.0), axis=1, keepdims=True)
    rank2 = jnp.sum(jnp.where(lane == i2, before, 0.0), axis=1, keepdims=True)
    cnt_ref[...] = cnt_ref[...] + jnp.sum(member, axis=0, keepdims=True)
    out = jnp.where(lane_i == 0, i1 - N_GROUPS,
          jnp.where(lane_i == 1, i2 - N_GROUPS,
          jnp.where(lane_i == 2, w1,
          jnp.where(lane_i == 3, w2,
          jnp.where(lane_i == 4, rank1, jnp.where(lane_i == 5, rank2, 0.0))))))
    o_ref[...] = out


def _router(h, gain, w_hi, w_lo, bias):
    n, d = h.shape
    return pl.pallas_call(
        _router_kernel, grid=(n // ROW_TILE,),
        in_specs=[pl.BlockSpec((ROW_TILE, d), lambda i: (i, 0)),
                  pl.BlockSpec((1, d), lambda i: (0, 0)),
                  pl.BlockSpec((d, LANES), lambda i: (0, 0)),
                  pl.BlockSpec((d, LANES), lambda i: (0, 0)),
                  pl.BlockSpec((1, LANES), lambda i: (0, 0))],
        out_specs=[pl.BlockSpec((ROW_TILE, LANES), lambda i: (i, 0)),
                   pl.BlockSpec((1, LANES), lambda i: (0, 0))],
        out_shape=[jax.ShapeDtypeStruct((n, LANES), F32),
                   jax.ShapeDtypeStruct((1, LANES), F32)],
        compiler_params=_cparams(1), name="moe_router",
    )(h, gain, w_hi, w_lo, bias)


def _expert_kernel(te_ref, tok_hbm, dst_hbm, h_hbm, g_ref, wg_ref, wu_ref, wd_ref, y_hbm,
                   xbuf0, xbuf1, ybuf0, ybuf1, tok0, tok1, dst0, dst1, tsem, dsem, gsem, ssem,
                   *, n_tiles):
    t = pl.program_id(0)
    R = EXPERT_TILE
    per = g_ref.shape[1] // LANES
    xbufs, ybufs, toks, dsts = (xbuf0, xbuf1), (ybuf0, ybuf1), (tok0, tok1), (dst0, dst1)

    def tok_copy(tile, s):
        return pltpu.make_async_copy(tok_hbm.at[tile], toks[s], tsem.at[s])

    def dst_copy(tile, s):
        return pltpu.make_async_copy(dst_hbm.at[tile], dsts[s], dsem.at[s])

    def start_gather(s):
        def body(g, c):
            for u in range(DMA_UNROLL):
                r = g * DMA_UNROLL + u
                src = h_hbm.at[pl.ds(pl.multiple_of(toks[s][r], per), per)]
                dst = xbufs[s].at[pl.ds(pl.multiple_of(r * per, per), per)]
                pltpu.make_async_copy(src, dst, gsem.at[s]).start(priority=u % DMA_PRIORITIES)
            return c
        lax.fori_loop(0, R // DMA_UNROLL, body, 0)

    def start_scatter(s):
        def body(g, c):
            for u in range(DMA_UNROLL):
                r = g * DMA_UNROLL + u
                src = ybufs[s].at[pl.ds(pl.multiple_of(r * per, per), per)]
                dst = y_hbm.at[pl.ds(pl.multiple_of(dsts[s][r], per), per)]
                pltpu.make_async_copy(src, dst, ssem.at[s]).start(priority=u % DMA_PRIORITIES)
            return c
        lax.fori_loop(0, R // DMA_UNROLL, body, 0)

    def wait_gather(s):
        pltpu.make_async_copy(h_hbm.at[pl.ds(0, R * per)], xbufs[s], gsem.at[s]).wait()

    def wait_scatter(s):
        pltpu.make_async_copy(ybufs[s], y_hbm.at[pl.ds(0, R * per)], ssem.at[s]).wait()

    def step(s):
        @pl.when(t == 0)
        def _():
            tok_copy(0, 0).start()
            tok_copy(0, 0).wait()
            start_gather(0)
            for k in range(1, min(3, n_tiles)):
                tok_copy(k, k % 2).start()
            for k in range(min(2, n_tiles)):
                dst_copy(k, k).start()

        @pl.when(t + 1 < n_tiles)
        def _():
            tok_copy(t + 1, 1 - s).wait()
            start_gather(1 - s)

        @pl.when(t + 3 < n_tiles)
        def _():
            tok_copy(t + 3, 1 - s).start()

        wait_gather(s)

        @pl.when(t >= 2)
        def _():
            wait_scatter(s)

        x = _from_row_tiles(xbufs[s], R, per)
        xn = _rms(x, g_ref[...]).astype(BF16)
        a = jnp.dot(xn, wg_ref[0, 0].astype(BF16), preferred_element_type=F32)
        u = jnp.dot(xn, wu_ref[0, 0].astype(BF16), preferred_element_type=F32)
        hid = (a / (1.0 + jnp.exp(-a))) * u
        y = jnp.dot(hid.astype(BF16), wd_ref[0, 0].astype(BF16), preferred_element_type=F32)
        _to_row_tiles(ybufs[s], y)

        dst_copy(t, s).wait()
        start_scatter(s)

        @pl.when(t + 2 < n_tiles)
        def _():
            dst_copy(t + 2, s).start()

        @pl.when(t == n_tiles - 1)
        def _():
            if n_tiles > 1:
                wait_scatter(1 - s)
            wait_scatter(s)

    for s in range(2):
        pl.when(lax.rem(t, 2) == s)(functools.partial(step, s))


def _expert_mlp(tile_expert, slot_token, slot_dest, h_tiles, gain, wg, wu, wd, layer, n_out_rows):
    n_tiles = slot_token.shape[0]
    d = gain.shape[1]
    per = d // LANES
    f = wg.shape[3]
    R = EXPERT_TILE
    kern = functools.partial(_expert_kernel, n_tiles=n_tiles)
    anyspec = pl.BlockSpec(memory_space=pl.ANY)
    buf = pltpu.VMEM((R * per, LANES), F32)
    idx = pltpu.SMEM((R,), jnp.int32)
    sems = pltpu.SemaphoreType.DMA((2,))
    return pl.pallas_call(
        kern,
        grid_spec=pltpu.PrefetchScalarGridSpec(
            num_scalar_prefetch=1, grid=(n_tiles,),
            in_specs=[anyspec, anyspec, anyspec,
                      pl.BlockSpec((1, d), lambda t, te: (0, 0)),
                      pl.BlockSpec((1, 1, d, f), lambda t, te: (layer, te[t], 0, 0)),
                      pl.BlockSpec((1, 1, d, f), lambda t, te: (layer, te[t], 0, 0)),
                      pl.BlockSpec((1, 1, f, d), lambda t, te: (layer, te[t], 0, 0))],
            out_specs=anyspec,
            scratch_shapes=[buf, buf, buf, buf, idx, idx, idx, idx, sems, sems, sems, sems]),
        out_shape=jax.ShapeDtypeStruct((n_out_rows * per, LANES), F32),
        compiler_params=_cparams(1), name="moe_expert",
    )(tile_expert, slot_token, slot_dest, h_tiles, gain, wg, wu, wd)


def _combine_kernel(ya_ref, yb_ref, h_ref, r_ref, o_ref):
    rows, d = h_ref.shape
    per = d // LANES
    route = r_ref[...]
    o_ref[...] = (h_ref[...] + route[:, 2:3] * _from_row_tiles(ya_ref, rows, per)
                  + route[:, 3:4] * _from_row_tiles(yb_ref, rows, per))


def _combine(y_tiles, h, route):
    n, d = h.shape
    per = d // LANES
    steps = n // ROW_TILE
    row = pl.BlockSpec((ROW_TILE, d), lambda i: (i, 0))
    tiles = lambda off: pl.BlockSpec((ROW_TILE * per, LANES), lambda i: (off + i, 0))
    return pl.pallas_call(
        _combine_kernel, grid=(steps,),
        in_specs=[tiles(0), tiles(steps), row, pl.BlockSpec((ROW_TILE, LANES), lambda i: (i, 0))],
        out_specs=row, out_shape=jax.ShapeDtypeStruct((n, d), F32),
        compiler_params=_cparams(1), name="moe_combine",
    )(y_tiles, y_tiles, h, route)


def _moe(h, h_tiles, gain, w_group, b_group, w_router, b_router, w_gate, w_up, w_down, layer):
    n, d = h.shape
    per = d // LANES
    pad = LANES - N_GROUPS - N_EXPERTS
    w_cat = jnp.concatenate([w_group, w_router, jnp.zeros((d, pad), F32)], axis=1)
    b_cat = jnp.concatenate([b_group, b_router, jnp.zeros((pad,), F32)])[None, :]
    w_hi = w_cat.astype(BF16)
    w_lo = (w_cat - w_hi.astype(F32)).astype(BF16)
    route, lane_counts = _router(h, gain, w_hi, w_lo, b_cat)

    e_flat = route[:, 0:2].astype(jnp.int32).reshape(-1)
    rank = route[:, 4:6].astype(jnp.int32).reshape(-1)
    counts = lane_counts[0, N_GROUPS:N_GROUPS + N_EXPERTS].astype(jnp.int32)
    ptiles = (counts + EXPERT_TILE - 1) // EXPERT_TILE
    tile_end = jnp.cumsum(ptiles)
    pstart = (tile_end - ptiles) * EXPERT_TILE
    pos = pstart[e_flat] + rank
    n_tiles = (2 * n) // EXPERT_TILE + N_EXPERTS
    n_slots = n_tiles * EXPERT_TILE
    slot_pair = jnp.full((n_slots,), -1, jnp.int32).at[pos].set(
        jnp.arange(2 * n, dtype=jnp.int32))
    is_pad = slot_pair < 0
    slot_token = jnp.where(is_pad, 0, slot_pair // 2) * per
    slot_dest = jnp.where(is_pad, 2 * n - 1 + jnp.cumsum(is_pad.astype(jnp.int32)),
                          (slot_pair % 2) * n + slot_pair // 2) * per
    tile_ids = jnp.arange(n_tiles, dtype=jnp.int32)
    tile_expert = jnp.minimum(
        jnp.sum((tile_ids[:, None] >= tile_end[None, :]).astype(jnp.int32), axis=1),
        N_EXPERTS - 1).astype(jnp.int32)
    n_valid = tile_end[-1]
    last_expert = tile_expert[jnp.maximum(n_valid - 1, 0)]
    tile_expert = jnp.where(tile_ids < n_valid, tile_expert, last_expert)

    y = _expert_mlp(tile_expert, slot_token.reshape(n_tiles, EXPERT_TILE),
                    slot_dest.reshape(n_tiles, EXPERT_TILE), h_tiles, gain,
                    w_gate, w_up, w_down, layer, n_slots)
    return _combine(y, h, route)


def kernel(x, meta_tokens, norm_mix, norm_ffn, diff_w_qkv, diff_q_gain, diff_k_gain,
           diff_lambda_q1, diff_lambda_k1, diff_lambda_q2, diff_lambda_k2, diff_sub_gain,
           diff_w_o, sb_w_qkv, sb_w_o, moe_w_group, moe_b_group, moe_w_router,
           moe_b_router, moe_w_gate, moe_w_up, moe_w_down):
    B, S, D = x.shape
    L = N_META + S
    Lp = -(-L // TQ) * TQ
    depth = norm_mix.shape[0]
    assert D == D_MODEL and TQ % Q_BLOCK == 0 and TQ % TK_MID == 0 and TK_BIG == TQ
    assert (B * Lp) % ROW_TILE == 0
    assert (2 * B * Lp) % EXPERT_TILE == 0

    meta = jnp.broadcast_to(meta_tokens.astype(x.dtype)[None], (B, N_META, D))
    h = jnp.concatenate([meta, x, jnp.zeros((B, Lp - L, D), x.dtype)], axis=1).reshape(B * Lp, D)

    blockdiag = (jnp.arange(MXU_DIM)[:, None] // DIFF_HEAD_DIM
                 == jnp.arange(MXU_DIM)[None, :] // DIFF_HEAD_DIM).astype(BF16)
    slopes = jnp.exp2(-8.0 * (jnp.arange(DIFF_HEADS, dtype=F32) + 1.0) / DIFF_HEADS)

    for i in range(depth):
        j = i // N_MIXERS
        gain = norm_mix[i][None, :]
        if i % N_MIXERS == 0:
            lambda_init = 0.8 - 0.6 * math.exp(-0.3 * i)
            scale = DIFF_HEAD_DIM ** -0.5 * LOG2E
            reps = D // DIFF_HEAD_DIM
            qg = (jnp.tile(diff_q_gain[j], reps) * scale)[None, :]
            kg = jnp.tile(diff_k_gain[j], reps)[None, :]
            qkv = _norm_qkv(h, gain, diff_w_qkv[j].astype(BF16), (qg, kg, blockdiag))
            lam_params = jnp.stack([diff_lambda_q1[j], diff_lambda_k1[j],
                                    diff_lambda_q2[j], diff_lambda_k2[j]])
            o = _diff_attention(qkv, slopes, lam_params, diff_sub_gain[j][None, :], B, Lp,
                                lambda_init)
            h, h_tiles = _proj_residual(o, diff_w_o[j].astype(BF16), h)
        else:
            scale = SB_HEAD_DIM ** -0.5 * LOG2E
            colscale = jnp.concatenate([jnp.full((D,), scale, F32), jnp.ones((2 * D,), F32)])
            w = (sb_w_qkv[j] * colscale[None, :]).astype(BF16)
            qkv = _norm_qkv(h, gain, w)
            o = _sb_attention(qkv, B, Lp)
            h, h_tiles = _proj_residual(o, sb_w_o[j].astype(BF16), h)
        h = _moe(h, h_tiles, norm_ffn[i][None, :], moe_w_group[i], moe_b_group[i], moe_w_router[i],
                 moe_b_router[i], moe_w_gate, moe_w_up, moe_w_down, i)
    return h.reshape(B, Lp, D)[:, N_META:N_META + S]
```
